```python
import jax, jax.numpy as jnp
from jax import lax
import numpy as np

D_MODEL = 2048
BATCH = 4
SEQ = 2048
DEPTH = 2

MEM_LEN = 256
HEAD_DIM = 128
CHUNK = 128
GMLP_WIDTH = D_MODEL // 2
POOL_WIDTH = D_MODEL // 4
CONV_WIDTH = D_MODEL - GMLP_WIDTH - POOL_WIDTH
MIX_WIDTH = GMLP_WIDTH + POOL_WIDTH + CONV_WIDTH
GMLP_HEADS = GMLP_WIDTH // HEAD_DIM
POOL_WINDOWS = (2, 4, 8, 16)
POOL_GROUPS = len(POOL_WINDOWS)
POOL_GROUP_WIDTH = POOL_WIDTH // POOL_GROUPS
MAX_WINDOW = max(POOL_WINDOWS)
CONV_K = 31
IN_COLS = 2 * GMLP_WIDTH + POOL_WIDTH + 2 * CONV_WIDTH
XATTN_HEADS = 4
XATTN_HEAD_DIM = D_MODEL // XATTN_HEADS
D_FF = 4 * D_MODEL
RMS_EPS = 1e-6
LN_EPS = 1e-5

kernel_name = "hybrid_gmlp_pool_conformer_block"


def rms_norm(x, g):
    xf = x.astype(jnp.float32)
    y = xf * lax.rsqrt(jnp.mean(xf * xf, axis=-1, keepdims=True) + RMS_EPS)
    return (y * g.astype(jnp.float32)).astype(x.dtype)


def layer_norm(x, g, b=None):
    xf = x.astype(jnp.float32)
    mu = jnp.mean(xf, axis=-1, keepdims=True)
    var = jnp.mean(jnp.square(xf - mu), axis=-1, keepdims=True)
    y = (xf - mu) * lax.rsqrt(var + LN_EPS) * g.astype(jnp.float32)
    if b is not None:
        y = y + b.astype(jnp.float32)
    return y.astype(x.dtype)


def spatial_gating(u, v, g_v, w_s, b_s):
    B, S, _ = u.shape
    u = u.reshape(B, S, GMLP_HEADS, HEAD_DIM)
    v = layer_norm(v.reshape(B, S, GMLP_HEADS, HEAD_DIM), g_v)
    mask = jnp.tril(jnp.ones((CHUNK, CHUNK), w_s.dtype))
    vc = v.reshape(B, S // CHUNK, CHUNK, GMLP_HEADS, HEAD_DIM)
    mixed = jnp.einsum('hts,bcshd->bcthd', w_s * mask, vc) + b_s.T[None, None, :, :, None]
    return (u * mixed.reshape(B, S, GMLP_HEADS, HEAD_DIM)).reshape(B, S, GMLP_WIDTH)


def multiscale_pool(p, w_pool, s_pool):
    B, S, _ = p.shape
    pf = p.astype(jnp.float32).reshape(B, S, POOL_GROUPS, POOL_GROUP_WIDTH)
    cs = jnp.cumsum(pf, axis=1)
    cs = jnp.pad(cs, ((0, 0), (MAX_WINDOW, 0), (0, 0), (0, 0)))
    pos = jnp.arange(S, dtype=jnp.float32)
    means = []
    for g, w in enumerate(POOL_WINDOWS):
        win = cs[:, MAX_WINDOW:, g] - cs[:, MAX_WINDOW - w:MAX_WINDOW - w + S, g]
        cnt = jnp.minimum(pos + 1.0, float(w))[None, :, None]
        means.append(win / cnt)
    pooled = jnp.stack(means, axis=2) - pf
    out = jnp.einsum('bsgc,gcd->bsgd', pooled, w_pool.astype(jnp.float32))
    out = out * s_pool.astype(jnp.float32).reshape(POOL_GROUPS, POOL_GROUP_WIDTH)
    return out.reshape(B, S, POOL_WIDTH).astype(p.dtype)


def conformer_conv(c_val, c_gate, w_dw, b_dw, ln_g, ln_b):
    h = c_val * jax.nn.sigmoid(c_gate)
    h = lax.conv_general_dilated(
        h, w_dw[:, None, :], window_strides=(1,), padding=[(CONV_K - 1, 0)],
        dimension_numbers=('NWC', 'WIO', 'NWC'), feature_group_count=CONV_WIDTH) + b_dw
    return jax.nn.silu(layer_norm(h, ln_g, ln_b))


def hybrid_mixer(h, w_in, w_out, g_v, w_s, b_s, w_pool, s_pool, w_dw, b_dw, ln_g, ln_b):
    z = h @ w_in
    cuts = (GMLP_WIDTH, 2 * GMLP_WIDTH, 2 * GMLP_WIDTH + POOL_WIDTH,
            2 * GMLP_WIDTH + POOL_WIDTH + CONV_WIDTH)
    z_a, p_b, c_val, c_gate = (z[..., :cuts[1]], z[..., cuts[1]:cuts[2]],
                               z[..., cuts[2]:cuts[3]], z[..., cuts[3]:])
    z_a = jax.nn.gelu(z_a)
    y_a = spatial_gating(z_a[..., :GMLP_WIDTH], z_a[..., GMLP_WIDTH:], g_v, w_s, b_s)
    y_b = multiscale_pool(p_b, w_pool, s_pool)
    y_c = conformer_conv(c_val, c_gate, w_dw, b_dw, ln_g, ln_b)
    return jnp.concatenate([y_a, y_b, y_c], axis=-1) @ w_out


def cross_attention(h, m, w_q, w_k, w_v, w_o):
    B, S, _ = h.shape
    q = (h @ w_q).reshape(B, S, XATTN_HEADS, XATTN_HEAD_DIM)
    k = (m @ w_k).reshape(B, MEM_LEN, XATTN_HEADS, XATTN_HEAD_DIM)
    v = (m @ w_v).reshape(B, MEM_LEN, XATTN_HEADS, XATTN_HEAD_DIM)
    scores = jnp.einsum('bshd,bmhd->bhsm', q.astype(jnp.float32), k.astype(jnp.float32))
    probs = jax.nn.softmax(scores * (XATTN_HEAD_DIM ** -0.5), axis=-1).astype(v.dtype)
    out = jnp.einsum('bhsm,bmhd->bshd', probs, v).reshape(B, S, D_MODEL)
    return out @ w_o


def setup_inputs(seed: int = 0) -> dict:
    key = jax.random.key(seed)
    ks = iter(jax.random.split(key, 32))
    f32 = jnp.float32

    def nrm(shape, scale):
        return jax.random.normal(next(ks), shape, f32) * scale

    def gain(shape):
        return 1.0 + 0.05 * jax.random.normal(next(ks), shape, f32)

    L = DEPTH
    return {
        "x": jax.random.normal(next(ks), (BATCH, SEQ, D_MODEL), f32),
        "mem": jax.random.normal(next(ks), (BATCH, MEM_LEN, D_MODEL), f32),
        "norm_mix_pre": gain((L, D_MODEL)),
        "norm_mix_post": gain((L, D_MODEL)),
        "w_in": nrm((L, D_MODEL, IN_COLS), D_MODEL ** -0.5),
        "w_out": nrm((L, MIX_WIDTH, D_MODEL), MIX_WIDTH ** -0.5),
        "gmlp_v_gain": gain((L, GMLP_HEADS, HEAD_DIM)),
        "w_spatial": nrm((L, GMLP_HEADS, CHUNK, CHUNK), CHUNK ** -0.5),
        "b_spatial": gain((L, GMLP_HEADS, CHUNK)),
        "w_pool": nrm((L, POOL_GROUPS, POOL_GROUP_WIDTH, POOL_GROUP_WIDTH), POOL_GROUP_WIDTH ** -0.5),
        "s_pool": gain((L, POOL_WIDTH)),
        "w_dw": nrm((L, CONV_K, CONV_WIDTH), CONV_K ** -0.5),
        "b_dw": nrm((L, CONV_WIDTH), 0.02),
        "conv_ln_g": gain((L, CONV_WIDTH)),
        "conv_ln_b": nrm((L, CONV_WIDTH), 0.02),
        "norm_xattn_pre": gain((L, D_MODEL)),
        "norm_mem": gain((L, D_MODEL)),
        "norm_xattn_post": gain((L, D_MODEL)),
        "w_q": nrm((L, D_MODEL, D_MODEL), D_MODEL ** -0.5),
        "w_k": nrm((L, D_MODEL, D_MODEL), D_MODEL ** -0.5),
        "w_v": nrm((L, D_MODEL, D_MODEL), D_MODEL ** -0.5),
        "w_o": nrm((L, D_MODEL, D_MODEL), D_MODEL ** -0.5),
        "norm_ffn_pre": gain((L, D_MODEL)),
        "norm_ffn_post": gain((L, D_MODEL)),
        "w_up": nrm((L, D_MODEL, D_FF), D_MODEL ** -0.5),
        "w_down": nrm((L, D_FF, D_MODEL), D_FF ** -0.5),
    }


def reference(x, mem, norm_mix_pre, norm_mix_post, w_in, w_out, gmlp_v_gain, w_spatial,
              b_spatial, w_pool, s_pool, w_dw, b_dw, conv_ln_g, conv_ln_b, norm_xattn_pre,
              norm_mem, norm_xattn_post, w_q, w_k, w_v, w_o, norm_ffn_pre, norm_ffn_post,
              w_up, w_down):
    for l in range(DEPTH):
        h = rms_norm(x, norm_mix_pre[l])
        h = hybrid_mixer(h, w_in[l], w_out[l], gmlp_v_gain[l], w_spatial[l], b_spatial[l],
                         w_pool[l], s_pool[l], w_dw[l], b_dw[l], conv_ln_g[l], conv_ln_b[l])
        x = x + rms_norm(h, norm_mix_post[l])
        h = rms_norm(x, norm_xattn_pre[l])
        m = rms_norm(mem, norm_mem[l])
        h = cross_attention(h, m, w_q[l], w_k[l], w_v[l], w_o[l])
        x = x + rms_norm(h, norm_xattn_post[l])
        h = rms_norm(x, norm_ffn_pre[l])
        h = jnp.square(jax.nn.relu(h @ w_up[l])) @ w_down[l]
        x = x + rms_norm(h, norm_ffn_post[l])
    return x
```

```python
import functools

import jax
import jax.numpy as jnp
from jax import lax
from jax.experimental import pallas as pl
from jax.experimental.pallas import tpu as pltpu

BF16 = jnp.bfloat16
F32 = jnp.float32

D_MODEL = 2048
HEAD_DIM = 128
CHUNK = 128
GMLP_WIDTH = 1024
GMLP_HEADS = GMLP_WIDTH // HEAD_DIM
POOL_WIDTH = 512
POOL_WINDOWS = (2, 4, 8, 16)
POOL_GROUP_WIDTH = POOL_WIDTH // len(POOL_WINDOWS)
CONV_WIDTH = 512
CONV_K = 31
IN_COLS = 2 * GMLP_WIDTH + POOL_WIDTH + 2 * CONV_WIDTH
XATTN_HEADS = 4
XATTN_HEAD_DIM = D_MODEL // XATTN_HEADS
RMS_EPS = 1e-6
LN_EPS = 1e-5

HALO = 32
VMEM_LIMIT_BYTES = 56 * 1024 * 1024


def _params(*semantics):
    return pltpu.CompilerParams(dimension_semantics=semantics, vmem_limit_bytes=VMEM_LIMIT_BYTES)


def _rms(x, g):
    return x * lax.rsqrt(jnp.mean(x * x, axis=-1, keepdims=True) + RMS_EPS) * g


def _layer_norm(x, g, b=None):
    mu = jnp.mean(x, axis=-1, keepdims=True)
    xc = x - mu
    var = jnp.mean(xc * xc, axis=-1, keepdims=True)
    y = xc * lax.rsqrt(var + LN_EPS) * g
    if b is not None:
        y = y + b
    return y


def _dot(a, b):
    return jnp.dot(a, b, preferred_element_type=F32)


def _layer_spec(layer, shape):
    zeros = (0,) * len(shape)
    return pl.BlockSpec((None,) + tuple(shape), lambda *_: (layer,) + zeros)


def _prenorm_kernel(x_ref, g_ref, o_ref):
    o_ref[...] = _rms(x_ref[...], g_ref[...]).astype(o_ref.dtype)


def _prenorm(x, g, layer, tm=512):
    t, d = x.shape
    return pl.pallas_call(
        _prenorm_kernel,
        grid=(t // tm,),
        in_specs=[pl.BlockSpec((tm, d), lambda i: (i, 0)), _layer_spec(layer, (1, d))],
        out_specs=pl.BlockSpec((tm, d), lambda i: (i, 0)),
        out_shape=jax.ShapeDtypeStruct((t, d), BF16),
        compiler_params=_params("parallel"),
        name="prenorm",
    )(x, g)


def _mixer_in_kernel(h_ref, w_ref, gv_ref, u_ref, vn_ref, p_ref, glu_ref):
    h = h_ref[...]
    u_ref[...] = jax.nn.gelu(_dot(h, w_ref[:, 0:GMLP_WIDTH])).astype(u_ref.dtype)
    v = jax.nn.gelu(_dot(h, w_ref[:, GMLP_WIDTH:2 * GMLP_WIDTH]))
    for hd in range(GMLP_HEADS):
        sl = slice(hd * HEAD_DIM, (hd + 1) * HEAD_DIM)
        vn_ref[:, sl] = _layer_norm(v[:, sl], gv_ref[:, sl]).astype(vn_ref.dtype)
    c0 = 2 * GMLP_WIDTH
    p_ref[...] = _dot(h, w_ref[:, c0:c0 + POOL_WIDTH])
    c1 = c0 + POOL_WIDTH
    c_val = _dot(h, w_ref[:, c1:c1 + CONV_WIDTH])
    c_gate = _dot(h, w_ref[:, c1 + CONV_WIDTH:c1 + 2 * CONV_WIDTH])
    glu_ref[...] = c_val * jax.nn.sigmoid(c_gate)


def _mixer_in(h, w_in, g_v, layer, tm=512):
    t, d = h.shape
    row = lambda n: pl.BlockSpec((tm, n), lambda i: (i, 0))
    return pl.pallas_call(
        _mixer_in_kernel,
        grid=(t // tm,),
        in_specs=[row(d), _layer_spec(layer, (d, IN_COLS)), _layer_spec(layer, (1, GMLP_WIDTH))],
        out_specs=[row(GMLP_WIDTH), row(GMLP_WIDTH), row(POOL_WIDTH), row(CONV_WIDTH)],
        out_shape=[
            jax.ShapeDtypeStruct((t, GMLP_WIDTH), BF16),
            jax.ShapeDtypeStruct((t, GMLP_WIDTH), BF16),
            jax.ShapeDtypeStruct((t, POOL_WIDTH), F32),
            jax.ShapeDtypeStruct((t, CONV_WIDTH), F32),
        ],
        compiler_params=_params("parallel"),
        name="mixer_in",
    )(h, w_in, g_v)


def _mixer_core_kernel(u_ref, vn_ref, p_ref, ph_ref, c_ref, ch_ref, ws_ref, bst_ref, wp_ref, sp_ref,
                       wdw_ref, bdw_ref, lng_ref, lnb_ref, y_ref, pext_ref, cext_ref, conv_ref,
                       *, tm, tiles_per_seq, conv_rows):
    first = pl.program_id(0) % tiles_per_seq == 0
    keep = jnp.where(first, 0.0, 1.0).astype(F32)

    tri = (lax.broadcasted_iota(jnp.int32, (CHUNK, CHUNK), 0)
           >= lax.broadcasted_iota(jnp.int32, (CHUNK, CHUNK), 1))
    for hd in range(GMLP_HEADS):
        w_mask = jnp.where(tri, ws_ref[hd], 0.0).astype(BF16)
        bias = bst_ref[:, hd:hd + 1]
        cols = slice(hd * HEAD_DIM, (hd + 1) * HEAD_DIM)
        for c in range(tm // CHUNK):
            rows = slice(c * CHUNK, (c + 1) * CHUNK)
            mixed = _dot(w_mask, vn_ref[rows, cols]) + bias
            y_ref[rows, cols] = (u_ref[rows, cols].astype(F32) * mixed).astype(y_ref.dtype)

    pext_ref[0:HALO, :] = ph_ref[...] * keep
    pext_ref[HALO:HALO + tm, :] = p_ref[...]
    pos = (lax.broadcasted_iota(jnp.int32, (tm, 1), 0)
           + (pl.program_id(0) % tiles_per_seq) * tm).astype(F32)
    for g, w in enumerate(POOL_WINDOWS):
        cols = slice(g * POOL_GROUP_WIDTH, (g + 1) * POOL_GROUP_WIDTH)
        win = pext_ref[HALO:HALO + tm, cols]
        for k in range(1, w):
            win = win + pext_ref[HALO - k:HALO - k + tm, cols]
        cnt = jnp.minimum(pos + 1.0, float(w))
        pooled = win / cnt - p_ref[:, cols]
        out = _dot(pooled.astype(BF16), wp_ref[g].astype(BF16)) * sp_ref[:, cols]
        y_ref[:, GMLP_WIDTH + g * POOL_GROUP_WIDTH:GMLP_WIDTH + (g + 1) * POOL_GROUP_WIDTH] = (
            out.astype(y_ref.dtype))

    cext_ref[0:HALO, :] = ch_ref[...] * keep
    cext_ref[HALO:HALO + tm, :] = c_ref[...]
    base = HALO - (CONV_K - 1)
    for r in range(tm // conv_rows):
        acc = jnp.broadcast_to(bdw_ref[...], (conv_rows, CONV_WIDTH))
        for j in range(CONV_K):
            start = r * conv_rows + base + j
            acc = acc + wdw_ref[j:j + 1, :] * cext_ref[start:start + conv_rows, :]
        conv_ref[r * conv_rows:(r + 1) * conv_rows, :] = acc
    yc = jax.nn.silu(_layer_norm(conv_ref[...], lng_ref[...], lnb_ref[...]))
    y_ref[:, GMLP_WIDTH + POOL_WIDTH:] = yc.astype(y_ref.dtype)


def _mixer_core(u, vn, p, glu, w_s, b_s_t, w_pool, s_pool, w_dw, b_dw, ln_g, ln_b, layer, seq, tm=512):
    t = u.shape[0]
    row = lambda n: pl.BlockSpec((tm, n), lambda i: (i, 0))
    halo = lambda n: pl.BlockSpec((HALO, n), lambda i: (jnp.maximum(i * (tm // HALO) - 1, 0), 0))
    kernel = functools.partial(_mixer_core_kernel, tm=tm, tiles_per_seq=seq // tm, conv_rows=32)
    return pl.pallas_call(
        kernel,
        grid=(t // tm,),
        in_specs=[
            row(GMLP_WIDTH), row(GMLP_WIDTH),
            row(POOL_WIDTH), halo(POOL_WIDTH),
            row(CONV_WIDTH), halo(CONV_WIDTH),
            _layer_spec(layer, (GMLP_HEADS, CHUNK, CHUNK)),
            _layer_spec(layer, (CHUNK, GMLP_HEADS)),
            _layer_spec(layer, (len(POOL_WINDOWS), POOL_GROUP_WIDTH, POOL_GROUP_WIDTH)),
            _layer_spec(layer, (1, POOL_WIDTH)),
            _layer_spec(layer, (CONV_K, CONV_WIDTH)),
            _layer_spec(layer, (1, CONV_WIDTH)),
            _layer_spec(layer, (1, CONV_WIDTH)),
            _layer_spec(layer, (1, CONV_WIDTH)),
        ],
        out_specs=row(D_MODEL),
        out_shape=jax.ShapeDtypeStruct((t, D_MODEL), BF16),
        scratch_shapes=[
            pltpu.VMEM((HALO + tm, POOL_WIDTH), F32),
            pltpu.VMEM((HALO + tm, CONV_WIDTH), F32),
            pltpu.VMEM((tm, CONV_WIDTH), F32),
        ],
        compiler_params=_params("parallel"),
        name="mixer_core",
    )(u, vn, p, p, glu, glu, w_s, b_s_t, w_pool, s_pool, w_dw, b_dw, ln_g, ln_b)


def _proj_res_kernel(a_ref, w_ref, x_ref, gpost_ref, gnext_ref, xo_ref, *ho_ref):
    h = _dot(a_ref[...], w_ref[...])
    xn = x_ref[...] + _rms(h, gpost_ref[...])
    xo_ref[...] = xn
    if ho_ref:
        ho_ref[0][...] = _rms(xn, gnext_ref[...]).astype(ho_ref[0].dtype)


def _proj_res(a, w, x, g_post, g_next, layer, next_layer, tm=512):
    t, k = a.shape
    d = x.shape[1]
    row = lambda n: pl.BlockSpec((tm, n), lambda i: (i, 0))
    emit_next = next_layer is not None
    out_specs = [row(d)] + ([row(d)] if emit_next else [])
    out_shape = [jax.ShapeDtypeStruct((t, d), F32)] + ([jax.ShapeDtypeStruct((t, d), BF16)] if emit_next else [])
    outs = pl.pallas_call(
        _proj_res_kernel,
        grid=(t // tm,),
        in_specs=[row(k), _layer_spec(layer, (k, d)), row(d), _layer_spec(layer, (1, d)),
                  _layer_spec(next_layer if emit_next else layer, (1, d))],
        out_specs=out_specs,
        out_shape=out_shape,
        compiler_params=_params("parallel"),
        name="proj_res",
    )(a, w, x, g_post, g_next)
    return (outs[0], outs[1]) if emit_next else (outs[0], None)


def _kv_kernel(m_ref, g_ref, wk_ref, wv_ref, k_ref, v_ref):
    m = _rms(m_ref[...], g_ref[...]).astype(BF16)
    k_ref[...] = _dot(m, wk_ref[...]).astype(k_ref.dtype)
    v_ref[...] = _dot(m, wv_ref[...]).astype(v_ref.dtype)


def _kv_proj(mem, g_mem, w_k, w_v, layer, tn=512):
    t, d = mem.shape
    full = pl.BlockSpec((t, d), lambda j: (0, 0))
    wcol = pl.BlockSpec((None, d, tn), lambda j: (layer, 0, j))
    ocol = pl.BlockSpec((t, tn), lambda j: (0, j))
    return pl.pallas_call(
        _kv_kernel,
        grid=(d // tn,),
        in_specs=[full, _layer_spec(layer, (1, d)), wcol, wcol],
        out_specs=[ocol, ocol],
        out_shape=[jax.ShapeDtypeStruct((t, d), BF16)] * 2,
        compiler_params=_params("parallel"),
        name="kv_proj",
    )(mem, g_mem, w_k, w_v)


def _attn_kernel(h_ref, wq_ref, k_ref, v_ref, o_ref):
    q = _dot(h_ref[...], wq_ref[...]).astype(BF16)
    scale = XATTN_HEAD_DIM ** -0.5
    for hd in range(XATTN_HEADS):
        cols = slice(hd * XATTN_HEAD_DIM, (hd + 1) * XATTN_HEAD_DIM)
        s = lax.dot_general(q[:, cols], k_ref[:, cols], (((1,), (1,)), ((), ())),
                            preferred_element_type=F32) * scale
        e = jnp.exp(s - jnp.max(s, axis=-1, keepdims=True))
        o = _dot(e.astype(BF16), v_ref[:, cols]) / jnp.sum(e, axis=-1, keepdims=True)
        o_ref[:, cols] = o.astype(o_ref.dtype)


def _attention(h, w_q, k, v, layer, seq, mem_len, tm=512):
    t, d = h.shape
    tiles_per_seq = seq // tm
    row = pl.BlockSpec((tm, d), lambda i: (i, 0))
    mem_rows = pl.BlockSpec((mem_len, d), lambda i: (i // tiles_per_seq, 0))
    return pl.pallas_call(
        _attn_kernel,
        grid=(t // tm,),
        in_specs=[row, _layer_spec(layer, (d, d)), mem_rows, mem_rows],
        out_specs=row,
        out_shape=jax.ShapeDtypeStruct((t, d), BF16),
        compiler_params=_params("parallel"),
        name="attention",
    )(h, w_q, k, v)


def _ffn_kernel(h_ref, wu_ref, wd_ref, x_ref, gpost_ref, gnext_ref, xo_ref, *rest):
    acc_ref = rest[-1]
    ho_ref = rest[:-1]
    f = pl.program_id(1)

    @pl.when(f == 0)
    def _():
        acc_ref[...] = jnp.zeros_like(acc_ref)

    a = jnp.square(jnp.maximum(_dot(h_ref[...], wu_ref[...]), 0.0))
    acc_ref[...] += _dot(a.astype(BF16), wd_ref[...])

    @pl.when(f == pl.num_programs(1) - 1)
    def _():
        xn = x_ref[...] + _rms(acc_ref[...], gpost_ref[...])
        xo_ref[...] = xn
        if ho_ref:
            ho_ref[0][...] = _rms(xn, gnext_ref[...]).astype(ho_ref[0].dtype)


def _ffn(h, w_up, w_down, x, g_post, g_next, layer, next_layer, tm=512, tf=512):
    t, d = h.shape
    d_ff = w_up.shape[-1]
    row = lambda n: pl.BlockSpec((tm, n), lambda i, f: (i, 0))
    emit_next = next_layer is not None
    out_specs = [row(d)] + ([row(d)] if emit_next else [])
    out_shape = [jax.ShapeDtypeStruct((t, d), F32)] + ([jax.ShapeDtypeStruct((t, d), BF16)] if emit_next else [])
    outs = pl.pallas_call(
        _ffn_kernel,
        grid=(t // tm, d_ff // tf),
        in_specs=[
            row(d),
            pl.BlockSpec((None, d, tf), lambda i, f: (layer, 0, f)),
            pl.BlockSpec((None, tf, d), lambda i, f: (layer, f, 0)),
            row(d),
            _layer_spec(layer, (1, d)),
            _layer_spec(next_layer if emit_next else layer, (1, d)),
        ],
        out_specs=out_specs,
        out_shape=out_shape,
        scratch_shapes=[pltpu.VMEM((tm, d), F32)],
        compiler_params=_params("parallel", "arbitrary"),
        name="ffn",
    )(h, w_up, w_down, x, g_post, g_next)
    return (outs[0], outs[1]) if emit_next else (outs[0], None)


def kernel(x, mem, norm_mix_pre, norm_mix_post, w_in, w_out, gmlp_v_gain, w_spatial, b_spatial, w_pool,
           s_pool, w_dw, b_dw, conv_ln_g, conv_ln_b, norm_xattn_pre, norm_mem, norm_xattn_post, w_q, w_k,
           w_v, w_o, norm_ffn_pre, norm_ffn_post, w_up, w_down):
    batch, seq, d = x.shape
    mem_len = mem.shape[1]
    depth = w_in.shape[0]
    t = batch * seq

    vec = lambda a: a.reshape(depth, 1, -1)
    norm_mix_pre, norm_mix_post = vec(norm_mix_pre), vec(norm_mix_post)
    norm_xattn_pre, norm_mem, norm_xattn_post = vec(norm_xattn_pre), vec(norm_mem), vec(norm_xattn_post)
    norm_ffn_pre, norm_ffn_post = vec(norm_ffn_pre), vec(norm_ffn_post)
    g_v, s_pool, b_dw = vec(gmlp_v_gain), vec(s_pool), vec(b_dw)
    conv_ln_g, conv_ln_b = vec(conv_ln_g), vec(conv_ln_b)
    b_s_t = jnp.swapaxes(b_spatial, 1, 2)

    w_in, w_out = w_in.astype(BF16), w_out.astype(BF16)
    w_q, w_k, w_v, w_o = w_q.astype(BF16), w_k.astype(BF16), w_v.astype(BF16), w_o.astype(BF16)
    w_up, w_down = w_up.astype(BF16), w_down.astype(BF16)

    xf = x.reshape(t, d)
    memf = mem.reshape(batch * mem_len, d)

    h = _prenorm(xf, norm_mix_pre, 0)
    for l in range(depth):
        u, vn, p, glu = _mixer_in(h, w_in, g_v, l)
        y = _mixer_core(u, vn, p, glu, w_spatial, b_s_t, w_pool, s_pool, w_dw, b_dw, conv_ln_g, conv_ln_b,
                        l, seq)
        xf, h = _proj_res(y, w_out, xf, norm_mix_post, norm_xattn_pre, l, l)
        k, v = _kv_proj(memf, norm_mem, w_k, w_v, l)
        a = _attention(h, w_q, k, v, l, seq, mem_len)
        xf, h = _proj_res(a, w_o, xf, norm_xattn_post, norm_ffn_pre, l, l)
        nxt = l + 1 if l + 1 < depth else None
        xf, h = _ffn(h, w_up, w_down, xf, norm_ffn_post, norm_mix_pre, l, nxt)
    return xf.reshape(batch, seq, d)
```

```python
import functools

import jax
import jax.numpy as jnp
from jax import lax
from jax.experimental import pallas as pl
from jax.experimental.pallas import tpu as pltpu

BF16 = jnp.bfloat16
F32 = jnp.float32

D_MODEL = 2048
HEAD_DIM = 128
CHUNK = 128
GMLP_WIDTH = 1024
GMLP_HEADS = GMLP_WIDTH // HEAD_DIM
POOL_WIDTH = 512
POOL_WINDOWS = (2, 4, 8, 16)
POOL_GROUP_WIDTH = POOL_WIDTH // len(POOL_WINDOWS)
CONV_WIDTH = 512
CONV_K = 31
IN_COLS = 2 * GMLP_WIDTH + POOL_WIDTH + 2 * CONV_WIDTH
XATTN_HEADS = 4
XATTN_HEAD_DIM = D_MODEL // XATTN_HEADS
RMS_EPS = 1e-6
LN_EPS = 1e-5

HALO = 32
SUBLANES = 8
VMEM_LIMIT_BYTES = 56 * 1024 * 1024
CONV_BLOCK_ROWS = 128
WEIGHT_CHUNK_ROWS = 256


def _params(*semantics):
    return pltpu.CompilerParams(dimension_semantics=semantics, vmem_limit_bytes=VMEM_LIMIT_BYTES)


def _rms(x, g):
    return x * lax.rsqrt(jnp.mean(x * x, axis=-1, keepdims=True) + RMS_EPS) * g


def _layer_norm(x, g, b=None):
    mu = jnp.mean(x, axis=-1, keepdims=True)
    xc = x - mu
    var = jnp.mean(xc * xc, axis=-1, keepdims=True)
    y = xc * lax.rsqrt(var + LN_EPS) * g
    if b is not None:
        y = y + b
    return y


def _dot(a, b):
    return jnp.dot(a, b, preferred_element_type=F32)


def _layer_spec(layer, shape):
    zeros = (0,) * len(shape)
    return pl.BlockSpec((None,) + tuple(shape), lambda *_: (layer,) + zeros)


def _load_cast_weight(w_hbm, layer, wbf_ref, stage_ref, sem):
    chunk_rows = stage_ref.shape[1]
    n_chunks = wbf_ref.shape[0] // chunk_rows

    def copy(c):
        return pltpu.make_async_copy(w_hbm.at[layer, pl.ds(c * chunk_rows, chunk_rows), :],
                                     stage_ref.at[c % 2], sem.at[c % 2])

    copy(0).start()
    for c in range(n_chunks):
        if c + 1 < n_chunks:
            copy(c + 1).start()
        copy(c).wait()
        wbf_ref[c * chunk_rows:(c + 1) * chunk_rows, :] = stage_ref[c % 2].astype(BF16)


def _weight_scratch(k, n):
    return [pltpu.VMEM((k, n), BF16), pltpu.VMEM((2, WEIGHT_CHUNK_ROWS, n), F32), pltpu.SemaphoreType.DMA((2,))]


HBM_SPEC = pl.BlockSpec(memory_space=pl.ANY)


def _side_cast_specs(w, layer, n_steps):
    _, rows, cols = w.shape
    rs = rows // n_steps
    in_spec = pl.BlockSpec((None, rs, cols), lambda i: (layer, i, 0))
    out_spec = pl.BlockSpec((rs, cols), lambda i: (i, 0))
    return in_spec, out_spec, jax.ShapeDtypeStruct((rows, cols), BF16)


def _prenorm_kernel(x_ref, g_ref, o_ref):
    o_ref[...] = _rms(x_ref[...], g_ref[...]).astype(o_ref.dtype)


def _prenorm(x, g, layer, tm=512):
    t, d = x.shape
    return pl.pallas_call(
        _prenorm_kernel,
        grid=(t // tm,),
        in_specs=[pl.BlockSpec((tm, d), lambda i: (i, 0)), _layer_spec(layer, (1, d))],
        out_specs=pl.BlockSpec((tm, d), lambda i: (i, 0)),
        out_shape=jax.ShapeDtypeStruct((t, d), BF16),
        compiler_params=_params("parallel"),
        name="prenorm",
    )(x, g)


def _mixer_in_kernel(h_ref, gv_ref, w_hbm, u_ref, vn_ref, p_ref, glu_ref, w_ref, stage_ref, sem, *, layer):
    @pl.when(pl.program_id(0) == 0)
    def _():
        _load_cast_weight(w_hbm, layer, w_ref, stage_ref, sem)

    h = h_ref[...]
    u_ref[...] = jax.nn.gelu(_dot(h, w_ref[:, 0:GMLP_WIDTH])).astype(u_ref.dtype)
    v = jax.nn.gelu(_dot(h, w_ref[:, GMLP_WIDTH:2 * GMLP_WIDTH]))
    for hd in range(GMLP_HEADS):
        sl = slice(hd * HEAD_DIM, (hd + 1) * HEAD_DIM)
        vn_ref[:, sl] = _layer_norm(v[:, sl], gv_ref[:, sl]).astype(vn_ref.dtype)
    c0 = 2 * GMLP_WIDTH
    p_ref[...] = _dot(h, w_ref[:, c0:c0 + POOL_WIDTH])
    c1 = c0 + POOL_WIDTH
    c_val = _dot(h, w_ref[:, c1:c1 + CONV_WIDTH])
    c_gate = _dot(h, w_ref[:, c1 + CONV_WIDTH:c1 + 2 * CONV_WIDTH])
    glu_ref[...] = c_val * jax.nn.sigmoid(c_gate)


def _mixer_in(h, w_in, g_v, layer, tm=512):
    t, d = h.shape
    row = lambda n: pl.BlockSpec((tm, n), lambda i: (i, 0))
    return pl.pallas_call(
        functools.partial(_mixer_in_kernel, layer=layer),
        grid=(t // tm,),
        in_specs=[row(d), _layer_spec(layer, (1, GMLP_WIDTH)), HBM_SPEC],
        out_specs=[row(GMLP_WIDTH), row(GMLP_WIDTH), row(POOL_WIDTH), row(CONV_WIDTH)],
        out_shape=[
            jax.ShapeDtypeStruct((t, GMLP_WIDTH), BF16),
            jax.ShapeDtypeStruct((t, GMLP_WIDTH), BF16),
            jax.ShapeDtypeStruct((t, POOL_WIDTH), F32),
            jax.ShapeDtypeStruct((t, CONV_WIDTH), F32),
        ],
        scratch_shapes=_weight_scratch(d, IN_COLS),
        compiler_params=_params("arbitrary"),
        name="mixer_in",
    )(h, g_v, w_in)


def _mixer_core_kernel(u_ref, vn_ref, p_ref, ph_ref, c_ref, ch_ref, ws_ref, bst_ref, wp_ref, sp_ref,
                       wdw_ref, bdw_ref, lng_ref, lnb_ref, y_ref, pext_ref, cext_ref, shift_ref, wtap_ref,
                       conv_ref, *, tm, tiles_per_seq, conv_rows):
    tile = pl.program_id(0)
    first = tile % tiles_per_seq == 0
    keep = jnp.where(first, 0.0, 1.0).astype(F32)

    tri = (lax.broadcasted_iota(jnp.int32, (CHUNK, CHUNK), 0)
           >= lax.broadcasted_iota(jnp.int32, (CHUNK, CHUNK), 1))
    for hd in range(GMLP_HEADS):
        w_mask = jnp.where(tri, ws_ref[hd], 0.0).astype(BF16)
        bias = bst_ref[:, hd:hd + 1]
        cols = slice(hd * HEAD_DIM, (hd + 1) * HEAD_DIM)
        for c in range(tm // CHUNK):
            rows = slice(c * CHUNK, (c + 1) * CHUNK)
            mixed = _dot(w_mask, vn_ref[rows, cols]) + bias
            y_ref[rows, cols] = (u_ref[rows, cols].astype(F32) * mixed).astype(y_ref.dtype)

    pext_ref[0:HALO, :] = ph_ref[...] * keep
    pext_ref[HALO:HALO + tm, :] = p_ref[...]
    pos = (lax.broadcasted_iota(jnp.int32, (tm, 1), 0) + (tile % tiles_per_seq) * tm).astype(F32)
    for g, w in enumerate(POOL_WINDOWS):
        cols = slice(g * POOL_GROUP_WIDTH, (g + 1) * POOL_GROUP_WIDTH)
        win = pext_ref[HALO:HALO + tm, cols]
        for k in range(1, w):
            win = win + pext_ref[HALO - k:HALO - k + tm, cols]
        cnt = jnp.minimum(pos + 1.0, float(w))
        pooled = win / cnt - p_ref[:, cols]
        out = _dot(pooled.astype(BF16), wp_ref[g].astype(BF16)) * sp_ref[:, cols]
        y_ref[:, GMLP_WIDTH + g * POOL_GROUP_WIDTH:GMLP_WIDTH + (g + 1) * POOL_GROUP_WIDTH] = (
            out.astype(y_ref.dtype))

    cext_ref[0:HALO, :] = ch_ref[...] * keep
    cext_ref[HALO:HALO + tm, :] = c_ref[...]
    for j in range(CONV_K):
        wtap_ref[j] = jnp.broadcast_to(wdw_ref[j:j + 1, :], (SUBLANES, CONV_WIDTH))
    base = HALO - (CONV_K - 1)
    bias = jnp.broadcast_to(bdw_ref[...], (SUBLANES, CONV_WIDTH))
    block_rows = shift_ref.shape[1] - HALO
    n_shift_rows = block_rows + HALO - SUBLANES
    for blk in range(tm // block_rows):
        row_blk = blk * block_rows
        for s in range(1, SUBLANES):
            shift_ref[s - 1, 0:n_shift_rows, :] = cext_ref[row_blk + s:row_blk + s + n_shift_rows, :]
        for r in range(block_rows // conv_rows):
            acc = [bias] * (conv_rows // SUBLANES)
            for j in range(CONV_K):
                s, start = (base + j) % SUBLANES, r * conv_rows + (base + j) // SUBLANES * SUBLANES
                w_tap = wtap_ref[j]
                for q in range(conv_rows // SUBLANES):
                    rows = slice(start + q * SUBLANES, start + (q + 1) * SUBLANES)
                    if s == 0:
                        tap = cext_ref[row_blk + rows.start:row_blk + rows.stop, :]
                    else:
                        tap = shift_ref[s - 1, rows, :]
                    acc[q] = acc[q] + w_tap * tap
            for q in range(conv_rows // SUBLANES):
                row0 = row_blk + r * conv_rows + q * SUBLANES
                conv_ref[row0:row0 + SUBLANES, :] = acc[q]
    yc = jax.nn.silu(_layer_norm(conv_ref[...], lng_ref[...], lnb_ref[...]))
    y_ref[:, GMLP_WIDTH + POOL_WIDTH:] = yc.astype(y_ref.dtype)


def _mixer_core(u, vn, p, glu, w_s, b_s_t, w_pool, s_pool, w_dw, b_dw, ln_g, ln_b, layer, seq, tm=512):
    t = u.shape[0]
    row = lambda n: pl.BlockSpec((tm, n), lambda i: (i, 0))
    halo = lambda n: pl.BlockSpec((HALO, n), lambda i: (jnp.maximum(i * (tm // HALO) - 1, 0), 0))
    kernel = functools.partial(_mixer_core_kernel, tm=tm, tiles_per_seq=seq // tm, conv_rows=32)
    return pl.pallas_call(
        kernel,
        grid=(t // tm,),
        in_specs=[
            row(GMLP_WIDTH), row(GMLP_WIDTH),
            row(POOL_WIDTH), halo(POOL_WIDTH),
            row(CONV_WIDTH), halo(CONV_WIDTH),
            _layer_spec(layer, (GMLP_HEADS, CHUNK, CHUNK)),
            _layer_spec(layer, (CHUNK, GMLP_HEADS)),
            _layer_spec(layer, (len(POOL_WINDOWS), POOL_GROUP_WIDTH, POOL_GROUP_WIDTH)),
            _layer_spec(layer, (1, POOL_WIDTH)),
            _layer_spec(layer, (CONV_K, CONV_WIDTH)),
            _layer_spec(layer, (1, CONV_WIDTH)),
            _layer_spec(layer, (1, CONV_WIDTH)),
            _layer_spec(layer, (1, CONV_WIDTH)),
        ],
        out_specs=row(D_MODEL),
        out_shape=jax.ShapeDtypeStruct((t, D_MODEL), BF16),
        scratch_shapes=[
            pltpu.VMEM((HALO + tm, POOL_WIDTH), F32),
            pltpu.VMEM((HALO + tm, CONV_WIDTH), F32),
            pltpu.VMEM((SUBLANES - 1, HALO + CONV_BLOCK_ROWS, CONV_WIDTH), F32),
            pltpu.VMEM((CONV_K, SUBLANES, CONV_WIDTH), F32),
            pltpu.VMEM((tm, CONV_WIDTH), F32),
        ],
        compiler_params=_params("parallel"),
        name="mixer_core",
    )(u, vn, p, p, glu, glu, w_s, b_s_t, w_pool, s_pool, w_dw, b_dw, ln_g, ln_b)


def _proj_res_kernel(a_ref, x_ref, gpost_ref, gnext_ref, w_hbm, *rest, layer, has_side):
    if has_side:
        side_ref, xo_ref, ho_ref, side_out_ref, w_ref, stage_ref, sem = rest
        side_out_ref[...] = side_ref[...].astype(side_out_ref.dtype)
    else:
        xo_ref, ho_ref, w_ref, stage_ref, sem = rest

    @pl.when(pl.program_id(0) == 0)
    def _():
        _load_cast_weight(w_hbm, layer, w_ref, stage_ref, sem)

    xn = x_ref[...] + _rms(_dot(a_ref[...], w_ref[...]), gpost_ref[...])
    xo_ref[...] = xn
    ho_ref[...] = _rms(xn, gnext_ref[...]).astype(ho_ref.dtype)


def _proj_res(a, w, x, g_post, g_next, layer, side=None, tm=512):
    t, k = a.shape
    d = x.shape[1]
    n = t // tm
    row = lambda c: pl.BlockSpec((tm, c), lambda i: (i, 0))
    in_specs = [row(k), row(d), _layer_spec(layer, (1, d)), _layer_spec(layer, (1, d)), HBM_SPEC]
    out_specs = [row(d), row(d)]
    out_shape = [jax.ShapeDtypeStruct((t, d), F32), jax.ShapeDtypeStruct((t, d), BF16)]
    args = [a, x, g_post, g_next, w]
    if side is not None:
        side_in, side_out, side_shape = _side_cast_specs(side, layer, n)
        in_specs.append(side_in)
        out_specs.append(side_out)
        out_shape.append(side_shape)
        args.append(side)
    return pl.pallas_call(
        functools.partial(_proj_res_kernel, layer=layer, has_side=side is not None),
        grid=(n,),
        in_specs=in_specs,
        out_specs=out_specs,
        out_shape=out_shape,
        scratch_shapes=_weight_scratch(k, d),
        compiler_params=_params("arbitrary"),
        name="proj_res",
    )(*args)


def _kv_kernel(m_ref, g_ref, wk_ref, wv_ref, k_ref, v_ref):
    m = _rms(m_ref[...], g_ref[...]).astype(BF16)
    k_ref[...] = _dot(m, wk_ref[...].astype(BF16)).astype(k_ref.dtype)
    v_ref[...] = _dot(m, wv_ref[...].astype(BF16)).astype(v_ref.dtype)


def _kv_proj(mem, g_mem, w_k, w_v, layer, tn=256):
    t, d = mem.shape
    full = pl.BlockSpec((t, d), lambda j: (0, 0))
    wcol = pl.BlockSpec((None, d, tn), lambda j: (layer, 0, j))
    ocol = pl.BlockSpec((t, tn), lambda j: (0, j))
    return pl.pallas_call(
        _kv_kernel,
        grid=(d // tn,),
        in_specs=[full, _layer_spec(layer, (1, d)), wcol, wcol],
        out_specs=[ocol, ocol],
        out_shape=[jax.ShapeDtypeStruct((t, d), BF16)] * 2,
        compiler_params=_params("parallel"),
        name="kv_proj",
    )(mem, g_mem, w_k, w_v)


def _attn_kernel(h_ref, k_ref, v_ref, w_hbm, side_ref, o_ref, side_out_ref, wq_ref, stage_ref, sem, *, layer):
    side_out_ref[...] = side_ref[...].astype(side_out_ref.dtype)

    @pl.when(pl.program_id(0) == 0)
    def _():
        _load_cast_weight(w_hbm, layer, wq_ref, stage_ref, sem)

    q = _dot(h_ref[...], wq_ref[...]).astype(BF16)
    scale = XATTN_HEAD_DIM ** -0.5
    for hd in range(XATTN_HEADS):
        cols = slice(hd * XATTN_HEAD_DIM, (hd + 1) * XATTN_HEAD_DIM)
        s = lax.dot_general(q[:, cols], k_ref[:, cols], (((1,), (1,)), ((), ())),
                            preferred_element_type=F32) * scale
        e = jnp.exp(s - jnp.max(s, axis=-1, keepdims=True))
        o = _dot(e.astype(BF16), v_ref[:, cols]) / jnp.sum(e, axis=-1, keepdims=True)
        o_ref[:, cols] = o.astype(o_ref.dtype)


def _attention(h, w_q, k, v, side, layer, seq, mem_len, tm=512):
    t, d = h.shape
    n = t // tm
    tiles_per_seq = seq // tm
    row = pl.BlockSpec((tm, d), lambda i: (i, 0))
    mem_rows = pl.BlockSpec((mem_len, d), lambda i: (i // tiles_per_seq, 0))
    side_in, side_out, side_shape = _side_cast_specs(side, layer, n)
    return pl.pallas_call(
        functools.partial(_attn_kernel, layer=layer),
        grid=(n,),
        in_specs=[row, mem_rows, mem_rows, HBM_SPEC, side_in],
        out_specs=[row, side_out],
        out_shape=[jax.ShapeDtypeStruct((t, d), BF16), side_shape],
        scratch_shapes=_weight_scratch(d, d),
        compiler_params=_params("arbitrary"),
        name="attention",
    )(h, k, v, w_q, side)


def _ffn_kernel(h_ref, wu_ref, wd_ref, x_ref, gpost_ref, gnext_ref, xo_ref, *rest):
    acc_ref = rest[-1]
    ho_ref = rest[:-1]
    f = pl.program_id(1)

    @pl.when(f == 0)
    def _():
        acc_ref[...] = jnp.zeros_like(acc_ref)

    a = jnp.square(jnp.maximum(_dot(h_ref[...], wu_ref[...]), 0.0))
    acc_ref[...] += _dot(a.astype(BF16), wd_ref[...])

    @pl.when(f == pl.num_programs(1) - 1)
    def _():
        xn = x_ref[...] + _rms(acc_ref[...], gpost_ref[...])
        xo_ref[...] = xn
        if ho_ref:
            ho_ref[0][...] = _rms(xn, gnext_ref[...]).astype(ho_ref[0].dtype)


def _ffn(h, w_up, w_down, x, g_post, g_next, layer, next_layer, tm=512, tf=1024):
    t, d = h.shape
    d_ff = w_up.shape[-1]
    row = lambda n: pl.BlockSpec((tm, n), lambda i, f: (i, 0))
    emit_next = next_layer is not None
    out_specs = [row(d)] + ([row(d)] if emit_next else [])
    out_shape = [jax.ShapeDtypeStruct((t, d), F32)] + ([jax.ShapeDtypeStruct((t, d), BF16)] if emit_next else [])
    outs = pl.pallas_call(
        _ffn_kernel,
        grid=(t // tm, d_ff // tf),
        in_specs=[
            row(d),
            pl.BlockSpec((d, tf), lambda i, f: (0, f)),
            pl.BlockSpec((tf, d), lambda i, f: (f, 0)),
            row(d),
            _layer_spec(layer, (1, d)),
            _layer_spec(next_layer if emit_next else layer, (1, d)),
        ],
        out_specs=out_specs,
        out_shape=out_shape,
        scratch_shapes=[pltpu.VMEM((tm, d), F32)],
        compiler_params=_params("parallel", "arbitrary"),
        name="ffn",
    )(h, w_up, w_down, x, g_post, g_next)
    return (outs[0], outs[1]) if emit_next else (outs[0], None)


def kernel(x, mem, norm_mix_pre, norm_mix_post, w_in, w_out, gmlp_v_gain, w_spatial, b_spatial, w_pool,
           s_pool, w_dw, b_dw, conv_ln_g, conv_ln_b, norm_xattn_pre, norm_mem, norm_xattn_post, w_q, w_k,
           w_v, w_o, norm_ffn_pre, norm_ffn_post, w_up, w_down):
    batch, seq, d = x.shape
    mem_len = mem.shape[1]
    depth = w_in.shape[0]
    t = batch * seq

    vec = lambda a: a.reshape(depth, 1, -1)
    norm_mix_pre, norm_mix_post = vec(norm_mix_pre), vec(norm_mix_post)
    norm_xattn_pre, norm_mem, norm_xattn_post = vec(norm_xattn_pre), vec(norm_mem), vec(norm_xattn_post)
    norm_ffn_pre, norm_ffn_post = vec(norm_ffn_pre), vec(norm_ffn_post)
    g_v, s_pool, b_dw = vec(gmlp_v_gain), vec(s_pool), vec(b_dw)
    conv_ln_g, conv_ln_b = vec(conv_ln_g), vec(conv_ln_b)
    b_s_t = jnp.swapaxes(b_spatial, 1, 2)

    xf = x.reshape(t, d)
    memf = mem.reshape(batch * mem_len, d)

    h = _prenorm(xf, norm_mix_pre, 0)
    for l in range(depth):
        u, vn, p, glu = _mixer_in(h, w_in, g_v, l)
        y = _mixer_core(u, vn, p, glu, w_spatial, b_s_t, w_pool, s_pool, w_dw, b_dw, conv_ln_g, conv_ln_b,
                        l, seq)
        xf, h = _proj_res(y, w_out, xf, norm_mix_post, norm_xattn_pre, l)
        k, v = _kv_proj(memf, norm_mem, w_k, w_v, l)
        a, w_up_bf = _attention(h, w_q, k, v, w_up, l, seq, mem_len)
        xf, h, w_down_bf = _proj_res(a, w_o, xf, norm_xattn_post, norm_ffn_pre, l, side=w_down)
        nxt = l + 1 if l + 1 < depth else None
        xf, h = _ffn(h, w_up_bf, w_down_bf, xf, norm_ffn_post, norm_mix_pre, l, nxt)
    return xf.reshape(batch, seq, d)
```

```python
import functools

import jax
import jax.numpy as jnp
from jax import lax
from jax.experimental import pallas as pl
from jax.experimental.pallas import tpu as pltpu

BF16 = jnp.bfloat16
F32 = jnp.float32

D_MODEL = 2048
HEAD_DIM = 128
CHUNK = 128
GMLP_WIDTH = 1024
GMLP_HEADS = GMLP_WIDTH // HEAD_DIM
POOL_WIDTH = 512
POOL_WINDOWS = (2, 4, 8, 16)
POOL_GROUP_WIDTH = POOL_WIDTH // len(POOL_WINDOWS)
CONV_WIDTH = 512
CONV_K = 31
IN_COLS = 2 * GMLP_WIDTH + POOL_WIDTH + 2 * CONV_WIDTH
XATTN_HEADS = 4
XATTN_HEAD_DIM = D_MODEL // XATTN_HEADS
RMS_EPS = 1e-6
LN_EPS = 1e-5

HALO = 32
SUBLANES = 8
VMEM_LIMIT_BYTES = 56 * 1024 * 1024
CONV_BLOCK_ROWS = 128
PROJ_ROW_CHUNKS = 4
WEIGHT_CHUNK_ROWS = 256


def _params(*semantics):
    return pltpu.CompilerParams(dimension_semantics=semantics, vmem_limit_bytes=VMEM_LIMIT_BYTES)


def _rms(x, g):
    return x * lax.rsqrt(jnp.mean(x * x, axis=-1, keepdims=True) + RMS_EPS) * g


def _layer_norm(x, g, b=None):
    mu = jnp.mean(x, axis=-1, keepdims=True)
    xc = x - mu
    var = jnp.mean(xc * xc, axis=-1, keepdims=True)
    y = xc * lax.rsqrt(var + LN_EPS) * g
    if b is not None:
        y = y + b
    return y


def _dot(a, b):
    return jnp.dot(a, b, preferred_element_type=F32)


def _layer_spec(layer, shape):
    zeros = (0,) * len(shape)
    return pl.BlockSpec((None,) + tuple(shape), lambda *_: (layer,) + zeros)


def _load_cast_weight(w_hbm, layer, wbf_ref, stage_ref, sem):
    chunk_rows = stage_ref.shape[1]
    n_chunks = wbf_ref.shape[0] // chunk_rows

    def copy(c):
        return pltpu.make_async_copy(w_hbm.at[layer, pl.ds(c * chunk_rows, chunk_rows), :],
                                     stage_ref.at[c % 2], sem.at[c % 2])

    copy(0).start()
    for c in range(n_chunks):
        if c + 1 < n_chunks:
            copy(c + 1).start()
        copy(c).wait()
        wbf_ref[c * chunk_rows:(c + 1) * chunk_rows, :] = stage_ref[c % 2].astype(BF16)


def _weight_scratch(k, n, chunk_rows=WEIGHT_CHUNK_ROWS):
    return [pltpu.VMEM((k, n), BF16), pltpu.VMEM((2, chunk_rows, n), F32), pltpu.SemaphoreType.DMA((2,))]


HBM_SPEC = pl.BlockSpec(memory_space=pl.ANY)


def _side_cast_specs(w, layer, n_steps):
    _, rows, cols = w.shape
    rs = rows // n_steps
    in_spec = pl.BlockSpec((None, rs, cols), lambda i: (layer, i, 0))
    out_spec = pl.BlockSpec((rs, cols), lambda i: (i, 0))
    return in_spec, out_spec, jax.ShapeDtypeStruct((rows, cols), BF16)


def _prenorm_kernel(x_ref, g_ref, o_ref):
    o_ref[...] = _rms(x_ref[...], g_ref[...]).astype(o_ref.dtype)


def _prenorm(x, g, layer, tm=512):
    t, d = x.shape
    return pl.pallas_call(
        _prenorm_kernel,
        grid=(t // tm,),
        in_specs=[pl.BlockSpec((tm, d), lambda i: (i, 0)), _layer_spec(layer, (1, d))],
        out_specs=pl.BlockSpec((tm, d), lambda i: (i, 0)),
        out_shape=jax.ShapeDtypeStruct((t, d), BF16),
        compiler_params=_params("parallel"),
        name="prenorm",
    )(x, g)


def _mixer_in_kernel(h_ref, gv_ref, w_hbm, side_ref, u_ref, vn_ref, p_ref, glu_ref, side_out_ref, w_ref,
                     stage_ref, sem, *, layer):
    side_out_ref[...] = side_ref[...].astype(side_out_ref.dtype)

    @pl.when(pl.program_id(0) == 0)
    def _():
        _load_cast_weight(w_hbm, layer, w_ref, stage_ref, sem)

    h = h_ref[...]
    u_ref[...] = jax.nn.gelu(_dot(h, w_ref[:, 0:GMLP_WIDTH])).astype(u_ref.dtype)
    v = jax.nn.gelu(_dot(h, w_ref[:, GMLP_WIDTH:2 * GMLP_WIDTH]))
    for hd in range(GMLP_HEADS):
        sl = slice(hd * HEAD_DIM, (hd + 1) * HEAD_DIM)
        vn_ref[:, sl] = _layer_norm(v[:, sl], gv_ref[:, sl]).astype(vn_ref.dtype)
    c0 = 2 * GMLP_WIDTH
    p_ref[...] = _dot(h, w_ref[:, c0:c0 + POOL_WIDTH])
    c1 = c0 + POOL_WIDTH
    c_val = _dot(h, w_ref[:, c1:c1 + CONV_WIDTH])
    c_gate = _dot(h, w_ref[:, c1 + CONV_WIDTH:c1 + 2 * CONV_WIDTH])
    glu_ref[...] = c_val * jax.nn.sigmoid(c_gate)


def _mixer_in(h, w_in, g_v, side, layer, tm=512):
    t, d = h.shape
    n = t // tm
    row = lambda c: pl.BlockSpec((tm, c), lambda i: (i, 0))
    side_in, side_out, side_shape = _side_cast_specs(side, layer, n)
    return pl.pallas_call(
        functools.partial(_mixer_in_kernel, layer=layer),
        grid=(n,),
        in_specs=[row(d), _layer_spec(layer, (1, GMLP_WIDTH)), HBM_SPEC, side_in],
        out_specs=[row(GMLP_WIDTH), row(GMLP_WIDTH), row(POOL_WIDTH), row(CONV_WIDTH), side_out],
        out_shape=[
            jax.ShapeDtypeStruct((t, GMLP_WIDTH), BF16),
            jax.ShapeDtypeStruct((t, GMLP_WIDTH), BF16),
            jax.ShapeDtypeStruct((t, POOL_WIDTH), F32),
            jax.ShapeDtypeStruct((t, CONV_WIDTH), F32),
            side_shape,
        ],
        scratch_shapes=_weight_scratch(d, IN_COLS, chunk_rows=128),
        compiler_params=_params("arbitrary"),
        name="mixer_in",
    )(h, g_v, w_in, side)


def _mixer_core_kernel(u_ref, vn_ref, p_ref, ph_ref, c_ref, ch_ref, ws_ref, bst_ref, wp_ref, sp_ref,
                       wdw_ref, bdw_ref, lng_ref, lnb_ref, y_ref, pext_ref, cext_ref, shift_ref, wtap_ref,
                       conv_ref, *, tm, tiles_per_seq, conv_rows):
    tile = pl.program_id(0)
    first = tile % tiles_per_seq == 0
    keep = jnp.where(first, 0.0, 1.0).astype(F32)

    tri = (lax.broadcasted_iota(jnp.int32, (CHUNK, CHUNK), 0)
           >= lax.broadcasted_iota(jnp.int32, (CHUNK, CHUNK), 1))
    for hd in range(GMLP_HEADS):
        w_mask = jnp.where(tri, ws_ref[hd], 0.0).astype(BF16)
        bias = bst_ref[:, hd:hd + 1]
        cols = slice(hd * HEAD_DIM, (hd + 1) * HEAD_DIM)
        for c in range(tm // CHUNK):
            rows = slice(c * CHUNK, (c + 1) * CHUNK)
            mixed = _dot(w_mask, vn_ref[rows, cols]) + bias
            y_ref[rows, cols] = (u_ref[rows, cols].astype(F32) * mixed).astype(y_ref.dtype)

    pext_ref[0:HALO, :] = ph_ref[...] * keep
    pext_ref[HALO:HALO + tm, :] = p_ref[...]
    pos = (lax.broadcasted_iota(jnp.int32, (tm, 1), 0) + (tile % tiles_per_seq) * tm).astype(F32)
    for g, w in enumerate(POOL_WINDOWS):
        cols = slice(g * POOL_GROUP_WIDTH, (g + 1) * POOL_GROUP_WIDTH)
        win = pext_ref[HALO:HALO + tm, cols]
        for k in range(1, w):
            win = win + pext_ref[HALO - k:HALO - k + tm, cols]
        cnt = jnp.minimum(pos + 1.0, float(w))
        pooled = win / cnt - p_ref[:, cols]
        out = _dot(pooled.astype(BF16), wp_ref[g].astype(BF16)) * sp_ref[:, cols]
        y_ref[:, GMLP_WIDTH + g * POOL_GROUP_WIDTH:GMLP_WIDTH + (g + 1) * POOL_GROUP_WIDTH] = (
            out.astype(y_ref.dtype))

    cext_ref[0:HALO, :] = ch_ref[...] * keep
    cext_ref[HALO:HALO + tm, :] = c_ref[...]
    for j in range(CONV_K):
        wtap_ref[j] = jnp.broadcast_to(wdw_ref[j:j + 1, :], (SUBLANES, CONV_WIDTH))
    base = HALO - (CONV_K - 1)
    bias = jnp.broadcast_to(bdw_ref[...], (SUBLANES, CONV_WIDTH))
    block_rows = shift_ref.shape[1] - HALO
    n_shift_rows = block_rows + HALO - SUBLANES
    for blk in range(tm // block_rows):
        row_blk = blk * block_rows
        for s in range(1, SUBLANES):
            shift_ref[s - 1, 0:n_shift_rows, :] = cext_ref[row_blk + s:row_blk + s + n_shift_rows, :]
        for r in range(block_rows // conv_rows):
            acc = [bias] * (conv_rows // SUBLANES)
            for j in range(CONV_K):
                s, start = (base + j) % SUBLANES, r * conv_rows + (base + j) // SUBLANES * SUBLANES
                w_tap = wtap_ref[j]
                for q in range(conv_rows // SUBLANES):
                    rows = slice(start + q * SUBLANES, start + (q + 1) * SUBLANES)
                    if s == 0:
                        tap = cext_ref[row_blk + rows.start:row_blk + rows.stop, :]
                    else:
                        tap = shift_ref[s - 1, rows, :]
                    acc[q] = acc[q] + w_tap * tap
            for q in range(conv_rows // SUBLANES):
                row0 = row_blk + r * conv_rows + q * SUBLANES
                conv_ref[row0:row0 + SUBLANES, :] = acc[q]
    yc = jax.nn.silu(_layer_norm(conv_ref[...], lng_ref[...], lnb_ref[...]))
    y_ref[:, GMLP_WIDTH + POOL_WIDTH:] = yc.astype(y_ref.dtype)


def _mixer_core(u, vn, p, glu, w_s, b_s_t, w_pool, s_pool, w_dw, b_dw, ln_g, ln_b, layer, seq, tm=512):
    t = u.shape[0]
    row = lambda n: pl.BlockSpec((tm, n), lambda i: (i, 0))
    halo = lambda n: pl.BlockSpec((HALO, n), lambda i: (jnp.maximum(i * (tm // HALO) - 1, 0), 0))
    kernel = functools.partial(_mixer_core_kernel, tm=tm, tiles_per_seq=seq // tm, conv_rows=32)
    return pl.pallas_call(
        kernel,
        grid=(t // tm,),
        in_specs=[
            row(GMLP_WIDTH), row(GMLP_WIDTH),
            row(POOL_WIDTH), halo(POOL_WIDTH),
            row(CONV_WIDTH), halo(CONV_WIDTH),
            _layer_spec(layer, (GMLP_HEADS, CHUNK, CHUNK)),
            _layer_spec(layer, (CHUNK, GMLP_HEADS)),
            _layer_spec(layer, (len(POOL_WINDOWS), POOL_GROUP_WIDTH, POOL_GROUP_WIDTH)),
            _layer_spec(layer, (1, POOL_WIDTH)),
            _layer_spec(layer, (CONV_K, CONV_WIDTH)),
            _layer_spec(layer, (1, CONV_WIDTH)),
            _layer_spec(layer, (1, CONV_WIDTH)),
            _layer_spec(layer, (1, CONV_WIDTH)),
        ],
        out_specs=row(D_MODEL),
        out_shape=jax.ShapeDtypeStruct((t, D_MODEL), BF16),
        scratch_shapes=[
            pltpu.VMEM((HALO + tm, POOL_WIDTH), F32),
            pltpu.VMEM((HALO + tm, CONV_WIDTH), F32),
            pltpu.VMEM((SUBLANES - 1, HALO + CONV_BLOCK_ROWS, CONV_WIDTH), F32),
            pltpu.VMEM((CONV_K, SUBLANES, CONV_WIDTH), F32),
            pltpu.VMEM((tm, CONV_WIDTH), F32),
        ],
        compiler_params=_params("parallel"),
        name="mixer_core",
    )(u, vn, p, p, glu, glu, w_s, b_s_t, w_pool, s_pool, w_dw, b_dw, ln_g, ln_b)


def _proj_res_kernel(a_ref, x_ref, gpost_ref, gnext_ref, w_hbm, xo_ref, ho_ref, w_ref, stage_ref, sem, *,
                     layer):
    @pl.when(pl.program_id(0) == 0)
    def _():
        _load_cast_weight(w_hbm, layer, w_ref, stage_ref, sem)

    chunk = a_ref.shape[0] // PROJ_ROW_CHUNKS
    for c in range(PROJ_ROW_CHUNKS):
        rows = slice(c * chunk, (c + 1) * chunk)
        xn = x_ref[rows, :] + _rms(_dot(a_ref[rows, :], w_ref[...]), gpost_ref[...])
        xo_ref[rows, :] = xn
        ho_ref[rows, :] = _rms(xn, gnext_ref[...]).astype(ho_ref.dtype)


def _proj_res(a, w, x, g_post, g_next, layer, tm=512):
    t, k = a.shape
    d = x.shape[1]
    row = lambda c: pl.BlockSpec((tm, c), lambda i: (i, 0))
    return pl.pallas_call(
        functools.partial(_proj_res_kernel, layer=layer),
        grid=(t // tm,),
        in_specs=[row(k), row(d), _layer_spec(layer, (1, d)), _layer_spec(layer, (1, d)), HBM_SPEC],
        out_specs=[row(d), row(d)],
        out_shape=[jax.ShapeDtypeStruct((t, d), F32), jax.ShapeDtypeStruct((t, d), BF16)],
        scratch_shapes=_weight_scratch(k, d),
        compiler_params=_params("arbitrary"),
        name="proj_res",
    )(a, x, g_post, g_next, w)


def _kv_kernel(m_ref, g_ref, wk_ref, wv_ref, k_ref, v_ref):
    m = _rms(m_ref[...], g_ref[...]).astype(BF16)
    k_ref[...] = _dot(m, wk_ref[...].astype(BF16)).astype(k_ref.dtype)
    v_ref[...] = _dot(m, wv_ref[...].astype(BF16)).astype(v_ref.dtype)


def _kv_proj(mem, g_mem, w_k, w_v, layer, tn=256):
    t, d = mem.shape
    full = pl.BlockSpec((t, d), lambda j: (0, 0))
    wcol = pl.BlockSpec((None, d, tn), lambda j: (layer, 0, j))
    ocol = pl.BlockSpec((t, tn), lambda j: (0, j))
    return pl.pallas_call(
        _kv_kernel,
        grid=(d // tn,),
        in_specs=[full, _layer_spec(layer, (1, d)), wcol, wcol],
        out_specs=[ocol, ocol],
        out_shape=[jax.ShapeDtypeStruct((t, d), BF16)] * 2,
        compiler_params=_params("parallel"),
        name="kv_proj",
    )(mem, g_mem, w_k, w_v)


def _attn_kernel(h_ref, k_ref, v_ref, w_hbm, side_ref, o_ref, side_out_ref, wq_ref, stage_ref, sem, *, layer):
    side_out_ref[...] = side_ref[...].astype(side_out_ref.dtype)

    @pl.when(pl.program_id(0) == 0)
    def _():
        _load_cast_weight(w_hbm, layer, wq_ref, stage_ref, sem)

    q = _dot(h_ref[...], wq_ref[...]).astype(BF16)
    scale = XATTN_HEAD_DIM ** -0.5
    for hd in range(XATTN_HEADS):
        cols = slice(hd * XATTN_HEAD_DIM, (hd + 1) * XATTN_HEAD_DIM)
        s = lax.dot_general(q[:, cols], k_ref[:, cols], (((1,), (1,)), ((), ())),
                            preferred_element_type=F32) * scale
        e = jnp.exp(s - jnp.max(s, axis=-1, keepdims=True))
        o = _dot(e.astype(BF16), v_ref[:, cols]) / jnp.sum(e, axis=-1, keepdims=True)
        o_ref[:, cols] = o.astype(o_ref.dtype)


def _attention(h, w_q, k, v, side, layer, seq, mem_len, tm=512):
    t, d = h.shape
    n = t // tm
    tiles_per_seq = seq // tm
    row = pl.BlockSpec((tm, d), lambda i: (i, 0))
    mem_rows = pl.BlockSpec((mem_len, d), lambda i: (i // tiles_per_seq, 0))
    side_in, side_out, side_shape = _side_cast_specs(side, layer, n)
    return pl.pallas_call(
        functools.partial(_attn_kernel, layer=layer),
        grid=(n,),
        in_specs=[row, mem_rows, mem_rows, HBM_SPEC, side_in],
        out_specs=[row, side_out],
        out_shape=[jax.ShapeDtypeStruct((t, d), BF16), side_shape],
        scratch_shapes=_weight_scratch(d, d),
        compiler_params=_params("arbitrary"),
        name="attention",
    )(h, k, v, w_q, side)


def _ffn_kernel(h_ref, x_ref, gpost_ref, gnext_ref, wu_hbm, wd_hbm, xo_ref, *rest, n_tiles, tf):
    *ho_ref, wu_buf, wd_buf, sem, acc_ref = rest
    i = pl.program_id(0)
    n_chunks = wd_hbm.shape[0] // tf

    def copies(f, slot):
        col0 = pl.multiple_of(f * tf, tf)
        return (pltpu.make_async_copy(wu_hbm.at[:, pl.ds(col0, tf)], wu_buf.at[slot], sem.at[0, slot]),
                pltpu.make_async_copy(wd_hbm.at[pl.ds(col0, tf), :], wd_buf.at[slot], sem.at[1, slot]))

    def start(f, slot):
        for cp in copies(f, slot):
            cp.start()

    def wait(f, slot):
        for cp in copies(f, slot):
            cp.wait()

    @pl.when(i == 0)
    def _():
        start(0, 0)

    def pair(p, carry, first=False):
        for slot in (0, 1):
            f = 2 * p + slot
            start((f + 1) % n_chunks, 1 - slot)
            wait(f, slot)
            a = jnp.square(jnp.maximum(_dot(h_ref[...], wu_buf[slot]), 0.0))
            part = _dot(a.astype(BF16), wd_buf[slot])
            if first and slot == 0:
                acc_ref[...] = part
            else:
                acc_ref[...] += part
        return carry

    pair(0, 0, first=True)
    lax.fori_loop(1, n_chunks // 2, pair, 0)

    xn = x_ref[...] + _rms(acc_ref[...], gpost_ref[...])
    xo_ref[...] = xn
    if ho_ref:
        ho_ref[0][...] = _rms(xn, gnext_ref[...]).astype(ho_ref[0].dtype)

    @pl.when(i == n_tiles - 1)
    def _():
        wait(0, 0)


def _ffn(h, w_up, w_down, x, g_post, g_next, layer, next_layer, tm=512, tf=1024):
    t, d = h.shape
    n = t // tm
    row = pl.BlockSpec((tm, d), lambda i: (i, 0))
    emit_next = next_layer is not None
    out_specs = [row] + ([row] if emit_next else [])
    out_shape = [jax.ShapeDtypeStruct((t, d), F32)] + ([jax.ShapeDtypeStruct((t, d), BF16)] if emit_next else [])
    outs = pl.pallas_call(
        functools.partial(_ffn_kernel, n_tiles=n, tf=tf),
        grid=(n,),
        in_specs=[row, row, _layer_spec(layer, (1, d)), _layer_spec(next_layer if emit_next else layer, (1, d)),
                  HBM_SPEC, HBM_SPEC],
        out_specs=out_specs,
        out_shape=out_shape,
        scratch_shapes=[pltpu.VMEM((2, d, tf), BF16), pltpu.VMEM((2, tf, d), BF16),
                        pltpu.SemaphoreType.DMA((2, 2)), pltpu.VMEM((tm, d), F32)],
        compiler_params=_params("arbitrary"),
        name="ffn",
    )(h, x, g_post, g_next, w_up, w_down)
    return (outs[0], outs[1]) if emit_next else (outs[0], None)


def kernel(x, mem, norm_mix_pre, norm_mix_post, w_in, w_out, gmlp_v_gain, w_spatial, b_spatial, w_pool,
           s_pool, w_dw, b_dw, conv_ln_g, conv_ln_b, norm_xattn_pre, norm_mem, norm_xattn_post, w_q, w_k,
           w_v, w_o, norm_ffn_pre, norm_ffn_post, w_up, w_down):
    batch, seq, d = x.shape
    mem_len = mem.shape[1]
    depth = w_in.shape[0]
    t = batch * seq

    vec = lambda a: a.reshape(depth, 1, -1)
    norm_mix_pre, norm_mix_post = vec(norm_mix_pre), vec(norm_mix_post)
    norm_xattn_pre, norm_mem, norm_xattn_post = vec(norm_xattn_pre), vec(norm_mem), vec(norm_xattn_post)
    norm_ffn_pre, norm_ffn_post = vec(norm_ffn_pre), vec(norm_ffn_post)
    g_v, s_pool, b_dw = vec(gmlp_v_gain), vec(s_pool), vec(b_dw)
    conv_ln_g, conv_ln_b = vec(conv_ln_g), vec(conv_ln_b)
    b_s_t = jnp.swapaxes(b_spatial, 1, 2)

    xf = x.reshape(t, d)
    memf = mem.reshape(batch * mem_len, d)

    h = _prenorm(xf, norm_mix_pre, 0)
    for l in range(depth):
        u, vn, p, glu, w_down_bf = _mixer_in(h, w_in, g_v, w_down, l)
        y = _mixer_core(u, vn, p, glu, w_spatial, b_s_t, w_pool, s_pool, w_dw, b_dw, conv_ln_g, conv_ln_b,
                        l, seq)
        xf, h = _proj_res(y, w_out, xf, norm_mix_post, norm_xattn_pre, l)
        k, v = _kv_proj(memf, norm_mem, w_k, w_v, l)
        a, w_up_bf = _attention(h, w_q, k, v, w_up, l, seq, mem_len)
        xf, h = _proj_res(a, w_o, xf, norm_xattn_post, norm_ffn_pre, l)
        nxt = l + 1 if l + 1 < depth else None
        xf, h = _ffn(h, w_up_bf, w_down_bf, xf, norm_ffn_post, norm_mix_pre, l, nxt)
    return xf.reshape(batch, seq, d)
```

```python
import functools

import jax
import jax.numpy as jnp
from jax import lax
from jax.experimental import pallas as pl
from jax.experimental.pallas import tpu as pltpu

BF16 = jnp.bfloat16
F32 = jnp.float32

D_MODEL = 2048
HEAD_DIM = 128
CHUNK = 128
GMLP_WIDTH = 1024
GMLP_HEADS = GMLP_WIDTH // HEAD_DIM
POOL_WIDTH = 512
POOL_WINDOWS = (2, 4, 8, 16)
POOL_GROUP_WIDTH = POOL_WIDTH // len(POOL_WINDOWS)
CONV_WIDTH = 512
CONV_K = 31
IN_COLS = 2 * GMLP_WIDTH + POOL_WIDTH + 2 * CONV_WIDTH
XATTN_HEADS = 4
XATTN_HEAD_DIM = D_MODEL // XATTN_HEADS
RMS_EPS = 1e-6
LN_EPS = 1e-5

HALO = 32
SUBLANES = 8
VMEM_LIMIT_BYTES = 56 * 1024 * 1024
CONV_BLOCK_ROWS = 128
PROJ_ROW_CHUNKS = 4
FFN_CHUNK = 512
WEIGHT_CHUNK_ROWS = 256


def _params(*semantics):
    return pltpu.CompilerParams(dimension_semantics=semantics, vmem_limit_bytes=VMEM_LIMIT_BYTES)


def _rms(x, g):
    return x * lax.rsqrt(jnp.mean(x * x, axis=-1, keepdims=True) + RMS_EPS) * g


def _layer_norm(x, g, b=None):
    mu = jnp.mean(x, axis=-1, keepdims=True)
    xc = x - mu
    var = jnp.mean(xc * xc, axis=-1, keepdims=True)
    y = xc * lax.rsqrt(var + LN_EPS) * g
    if b is not None:
        y = y + b
    return y


def _dot(a, b):
    return jnp.dot(a, b, preferred_element_type=F32)


def _layer_spec(layer, shape):
    zeros = (0,) * len(shape)
    return pl.BlockSpec((None,) + tuple(shape), lambda *_: (layer,) + zeros)


def _load_cast_weight(w_hbm, layer, wbf_ref, stage_ref, sem):
    chunk_rows = stage_ref.shape[1]
    n_chunks = wbf_ref.shape[0] // chunk_rows

    def copy(c):
        return pltpu.make_async_copy(w_hbm.at[layer, pl.ds(c * chunk_rows, chunk_rows), :],
                                     stage_ref.at[c % 2], sem.at[c % 2])

    copy(0).start()
    for c in range(n_chunks):
        if c + 1 < n_chunks:
            copy(c + 1).start()
        copy(c).wait()
        wbf_ref[c * chunk_rows:(c + 1) * chunk_rows, :] = stage_ref[c % 2].astype(BF16)


def _weight_scratch(k, n, chunk_rows=WEIGHT_CHUNK_ROWS):
    return [pltpu.VMEM((k, n), BF16), pltpu.VMEM((2, chunk_rows, n), F32), pltpu.SemaphoreType.DMA((2,))]


HBM_SPEC = pl.BlockSpec(memory_space=pl.ANY)


def _side_cast_specs(w, layer, n_steps, col_tile=None):
    _, rows, cols = w.shape
    rs = rows // n_steps
    in_spec = pl.BlockSpec((None, rs, cols), lambda i: (layer, i, 0))
    if col_tile is None:
        return in_spec, pl.BlockSpec((rs, cols), lambda i: (i, 0)), jax.ShapeDtypeStruct((rows, cols), BF16)
    n_tiles = cols // col_tile
    out_spec = pl.BlockSpec((n_tiles, rs, col_tile), lambda i: (0, i, 0))
    return in_spec, out_spec, jax.ShapeDtypeStruct((n_tiles, rows, col_tile), BF16)


def _side_cast(side_ref, side_out_ref):
    if len(side_out_ref.shape) == 2:
        side_out_ref[...] = side_ref[...].astype(BF16)
    else:
        n_tiles, _, col_tile = side_out_ref.shape
        for j in range(n_tiles):
            side_out_ref[j] = side_ref[:, j * col_tile:(j + 1) * col_tile].astype(BF16)


def _prenorm_kernel(x_ref, g_ref, o_ref):
    o_ref[...] = _rms(x_ref[...], g_ref[...]).astype(o_ref.dtype)


def _prenorm(x, g, layer, tm=512):
    t, d = x.shape
    return pl.pallas_call(
        _prenorm_kernel,
        grid=(t // tm,),
        in_specs=[pl.BlockSpec((tm, d), lambda i: (i, 0)), _layer_spec(layer, (1, d))],
        out_specs=pl.BlockSpec((tm, d), lambda i: (i, 0)),
        out_shape=jax.ShapeDtypeStruct((t, d), BF16),
        compiler_params=_params("parallel"),
        name="prenorm",
    )(x, g)


def _mixer_in_kernel(h_ref, gv_ref, w_hbm, side_ref, u_ref, vn_ref, p_ref, glu_ref, side_out_ref, w_ref,
                     stage_ref, sem, *, layer):
    _side_cast(side_ref, side_out_ref)

    @pl.when(pl.program_id(0) == 0)
    def _():
        _load_cast_weight(w_hbm, layer, w_ref, stage_ref, sem)

    h = h_ref[...]
    u_ref[...] = jax.nn.gelu(_dot(h, w_ref[:, 0:GMLP_WIDTH])).astype(u_ref.dtype)
    v = jax.nn.gelu(_dot(h, w_ref[:, GMLP_WIDTH:2 * GMLP_WIDTH]))
    for hd in range(GMLP_HEADS):
        sl = slice(hd * HEAD_DIM, (hd + 1) * HEAD_DIM)
        vn_ref[:, sl] = _layer_norm(v[:, sl], gv_ref[:, sl]).astype(vn_ref.dtype)
    c0 = 2 * GMLP_WIDTH
    p_ref[...] = _dot(h, w_ref[:, c0:c0 + POOL_WIDTH])
    c1 = c0 + POOL_WIDTH
    c_val = _dot(h, w_ref[:, c1:c1 + CONV_WIDTH])
    c_gate = _dot(h, w_ref[:, c1 + CONV_WIDTH:c1 + 2 * CONV_WIDTH])
    glu_ref[...] = c_val * jax.nn.sigmoid(c_gate)


def _mixer_in(h, w_in, g_v, side, layer, tm=512):
    t, d = h.shape
    n = t // tm
    row = lambda c: pl.BlockSpec((tm, c), lambda i: (i, 0))
    side_in, side_out, side_shape = _side_cast_specs(side, layer, n)
    return pl.pallas_call(
        functools.partial(_mixer_in_kernel, layer=layer),
        grid=(n,),
        in_specs=[row(d), _layer_spec(layer, (1, GMLP_WIDTH)), HBM_SPEC, side_in],
        out_specs=[row(GMLP_WIDTH), row(GMLP_WIDTH), row(POOL_WIDTH), row(CONV_WIDTH), side_out],
        out_shape=[
            jax.ShapeDtypeStruct((t, GMLP_WIDTH), BF16),
            jax.ShapeDtypeStruct((t, GMLP_WIDTH), BF16),
            jax.ShapeDtypeStruct((t, POOL_WIDTH), F32),
            jax.ShapeDtypeStruct((t, CONV_WIDTH), F32),
            side_shape,
        ],
        scratch_shapes=_weight_scratch(d, IN_COLS, chunk_rows=128),
        compiler_params=_params("arbitrary"),
        name="mixer_in",
    )(h, g_v, w_in, side)


def _mixer_core_kernel(u_ref, vn_ref, p_ref, ph_ref, c_ref, ch_ref, ws_ref, bst_ref, wp_ref, sp_ref,
                       wdw_ref, bdw_ref, lng_ref, lnb_ref, y_ref, pext_ref, cext_ref, shift_ref, wtap_ref,
                       conv_ref, *, tm, tiles_per_seq, conv_rows):
    tile = pl.program_id(0)
    first = tile % tiles_per_seq == 0
    keep = jnp.where(first, 0.0, 1.0).astype(F32)

    tri = (lax.broadcasted_iota(jnp.int32, (CHUNK, CHUNK), 0)
           >= lax.broadcasted_iota(jnp.int32, (CHUNK, CHUNK), 1))
    for hd in range(GMLP_HEADS):
        w_mask = jnp.where(tri, ws_ref[hd], 0.0).astype(BF16)
        bias = bst_ref[:, hd:hd + 1]
        cols = slice(hd * HEAD_DIM, (hd + 1) * HEAD_DIM)
        for c in range(tm // CHUNK):
            rows = slice(c * CHUNK, (c + 1) * CHUNK)
            mixed = _dot(w_mask, vn_ref[rows, cols]) + bias
            y_ref[rows, cols] = (u_ref[rows, cols].astype(F32) * mixed).astype(y_ref.dtype)

    pext_ref[0:HALO, :] = ph_ref[...] * keep
    pext_ref[HALO:HALO + tm, :] = p_ref[...]
    pos = (lax.broadcasted_iota(jnp.int32, (tm, 1), 0) + (tile % tiles_per_seq) * tm).astype(F32)
    for g, w in enumerate(POOL_WINDOWS):
        cols = slice(g * POOL_GROUP_WIDTH, (g + 1) * POOL_GROUP_WIDTH)
        win = pext_ref[HALO:HALO + tm, cols]
        for k in range(1, w):
            win = win + pext_ref[HALO - k:HALO - k + tm, cols]
        cnt = jnp.minimum(pos + 1.0, float(w))
        pooled = win / cnt - p_ref[:, cols]
        out = _dot(pooled.astype(BF16), wp_ref[g].astype(BF16)) * sp_ref[:, cols]
        y_ref[:, GMLP_WIDTH + g * POOL_GROUP_WIDTH:GMLP_WIDTH + (g + 1) * POOL_GROUP_WIDTH] = (
            out.astype(y_ref.dtype))

    cext_ref[0:HALO, :] = ch_ref[...] * keep
    cext_ref[HALO:HALO + tm, :] = c_ref[...]
    for j in range(CONV_K):
        wtap_ref[j] = jnp.broadcast_to(wdw_ref[j:j + 1, :], (SUBLANES, CONV_WIDTH))
    base = HALO - (CONV_K - 1)
    bias = jnp.broadcast_to(bdw_ref[...], (SUBLANES, CONV_WIDTH))
    block_rows = shift_ref.shape[1] - HALO
    n_shift_rows = block_rows + HALO - SUBLANES
    for blk in range(tm // block_rows):
        row_blk = blk * block_rows
        for s in range(1, SUBLANES):
            shift_ref[s - 1, 0:n_shift_rows, :] = cext_ref[row_blk + s:row_blk + s + n_shift_rows, :]
        for r in range(block_rows // conv_rows):
            acc = [bias] * (conv_rows // SUBLANES)
            for j in range(CONV_K):
                s, start = (base + j) % SUBLANES, r * conv_rows + (base + j) // SUBLANES * SUBLANES
                w_tap = wtap_ref[j]
                for q in range(conv_rows // SUBLANES):
                    rows = slice(start + q * SUBLANES, start + (q + 1) * SUBLANES)
                    if s == 0:
                        tap = cext_ref[row_blk + rows.start:row_blk + rows.stop, :]
                    else:
                        tap = shift_ref[s - 1, rows, :]
                    acc[q] = acc[q] + w_tap * tap
            for q in range(conv_rows // SUBLANES):
                row0 = row_blk + r * conv_rows + q * SUBLANES
                conv_ref[row0:row0 + SUBLANES, :] = acc[q]
    yc = jax.nn.silu(_layer_norm(conv_ref[...], lng_ref[...], lnb_ref[...]))
    y_ref[:, GMLP_WIDTH + POOL_WIDTH:] = yc.astype(y_ref.dtype)


def _mixer_core(u, vn, p, glu, w_s, b_s_t, w_pool, s_pool, w_dw, b_dw, ln_g, ln_b, layer, seq, tm=512):
    t = u.shape[0]
    row = lambda n: pl.BlockSpec((tm, n), lambda i: (i, 0))
    halo = lambda n: pl.BlockSpec((HALO, n), lambda i: (jnp.maximum(i * (tm // HALO) - 1, 0), 0))
    kernel = functools.partial(_mixer_core_kernel, tm=tm, tiles_per_seq=seq // tm, conv_rows=32)
    return pl.pallas_call(
        kernel,
        grid=(t // tm,),
        in_specs=[
            row(GMLP_WIDTH), row(GMLP_WIDTH),
            row(POOL_WIDTH), halo(POOL_WIDTH),
            row(CONV_WIDTH), halo(CONV_WIDTH),
            _layer_spec(layer, (GMLP_HEADS, CHUNK, CHUNK)),
            _layer_spec(layer, (CHUNK, GMLP_HEADS)),
            _layer_spec(layer, (len(POOL_WINDOWS), POOL_GROUP_WIDTH, POOL_GROUP_WIDTH)),
            _layer_spec(layer, (1, POOL_WIDTH)),
            _layer_spec(layer, (CONV_K, CONV_WIDTH)),
            _layer_spec(layer, (1, CONV_WIDTH)),
            _layer_spec(layer, (1, CONV_WIDTH)),
            _layer_spec(layer, (1, CONV_WIDTH)),
        ],
        out_specs=row(D_MODEL),
        out_shape=jax.ShapeDtypeStruct((t, D_MODEL), BF16),
        scratch_shapes=[
            pltpu.VMEM((HALO + tm, POOL_WIDTH), F32),
            pltpu.VMEM((HALO + tm, CONV_WIDTH), F32),
            pltpu.VMEM((SUBLANES - 1, HALO + CONV_BLOCK_ROWS, CONV_WIDTH), F32),
            pltpu.VMEM((CONV_K, SUBLANES, CONV_WIDTH), F32),
            pltpu.VMEM((tm, CONV_WIDTH), F32),
        ],
        compiler_params=_params("parallel"),
        name="mixer_core",
    )(u, vn, p, p, glu, glu, w_s, b_s_t, w_pool, s_pool, w_dw, b_dw, ln_g, ln_b)


def _proj_res_kernel(a_ref, x_ref, gpost_ref, gnext_ref, w_hbm, xo_ref, ho_ref, w_ref, stage_ref, sem, *,
                     layer):
    @pl.when(pl.program_id(0) == 0)
    def _():
        _load_cast_weight(w_hbm, layer, w_ref, stage_ref, sem)

    chunk = a_ref.shape[0] // PROJ_ROW_CHUNKS
    for c in range(PROJ_ROW_CHUNKS):
        rows = slice(c * chunk, (c + 1) * chunk)
        xn = x_ref[rows, :] + _rms(_dot(a_ref[rows, :], w_ref[...]), gpost_ref[...])
        xo_ref[rows, :] = xn
        ho_ref[rows, :] = _rms(xn, gnext_ref[...]).astype(ho_ref.dtype)


def _proj_res(a, w, x, g_post, g_next, layer, tm=512):
    t, k = a.shape
    d = x.shape[1]
    row = lambda c: pl.BlockSpec((tm, c), lambda i: (i, 0))
    return pl.pallas_call(
        functools.partial(_proj_res_kernel, layer=layer),
        grid=(t // tm,),
        in_specs=[row(k), row(d), _layer_spec(layer, (1, d)), _layer_spec(layer, (1, d)), HBM_SPEC],
        out_specs=[row(d), row(d)],
        out_shape=[jax.ShapeDtypeStruct((t, d), F32), jax.ShapeDtypeStruct((t, d), BF16)],
        scratch_shapes=_weight_scratch(k, d),
        compiler_params=_params("arbitrary"),
        name="proj_res",
    )(a, x, g_post, g_next, w)


def _kv_kernel(m_ref, g_ref, wk_ref, wv_ref, k_ref, v_ref):
    m = _rms(m_ref[...], g_ref[...]).astype(BF16)
    k_ref[...] = _dot(m, wk_ref[...].astype(BF16)).astype(k_ref.dtype)
    v_ref[...] = _dot(m, wv_ref[...].astype(BF16)).astype(v_ref.dtype)


def _kv_proj(mem, g_mem, w_k, w_v, layer, tn=256):
    t, d = mem.shape
    full = pl.BlockSpec((t, d), lambda j: (0, 0))
    wcol = pl.BlockSpec((None, d, tn), lambda j: (layer, 0, j))
    ocol = pl.BlockSpec((t, tn), lambda j: (0, j))
    return pl.pallas_call(
        _kv_kernel,
        grid=(d // tn,),
        in_specs=[full, _layer_spec(layer, (1, d)), wcol, wcol],
        out_specs=[ocol, ocol],
        out_shape=[jax.ShapeDtypeStruct((t, d), BF16)] * 2,
        compiler_params=_params("parallel"),
        name="kv_proj",
    )(mem, g_mem, w_k, w_v)


def _attn_kernel(h_ref, k_ref, v_ref, w_hbm, side_ref, o_ref, side_out_ref, wq_ref, stage_ref, sem, *, layer):
    _side_cast(side_ref, side_out_ref)

    @pl.when(pl.program_id(0) == 0)
    def _():
        _load_cast_weight(w_hbm, layer, wq_ref, stage_ref, sem)

    q = _dot(h_ref[...], wq_ref[...]).astype(BF16)
    scale = XATTN_HEAD_DIM ** -0.5
    for hd in range(XATTN_HEADS):
        cols = slice(hd * XATTN_HEAD_DIM, (hd + 1) * XATTN_HEAD_DIM)
        s = lax.dot_general(q[:, cols], k_ref[:, cols], (((1,), (1,)), ((), ())),
                            preferred_element_type=F32) * scale
        e = jnp.exp(s - jnp.max(s, axis=-1, keepdims=True))
        o = _dot(e.astype(BF16), v_ref[:, cols]) / jnp.sum(e, axis=-1, keepdims=True)
        o_ref[:, cols] = o.astype(o_ref.dtype)


def _attention(h, w_q, k, v, side, layer, seq, mem_len, tm=512):
    t, d = h.shape
    n = t // tm
    tiles_per_seq = seq // tm
    row = pl.BlockSpec((tm, d), lambda i: (i, 0))
    mem_rows = pl.BlockSpec((mem_len, d), lambda i: (i // tiles_per_seq, 0))
    side_in, side_out, side_shape = _side_cast_specs(side, layer, n, col_tile=FFN_CHUNK)
    return pl.pallas_call(
        functools.partial(_attn_kernel, layer=layer),
        grid=(n,),
        in_specs=[row, mem_rows, mem_rows, HBM_SPEC, side_in],
        out_specs=[row, side_out],
        out_shape=[jax.ShapeDtypeStruct((t, d), BF16), side_shape],
        scratch_shapes=_weight_scratch(d, d),
        compiler_params=_params("arbitrary"),
        name="attention",
    )(h, k, v, w_q, side)


def _ffn_kernel(h_ref, gpost_ref, gnext_ref, x_hbm, wu_hbm, wd_hbm, xo_ref, *rest, n_tiles):
    *ho_ref, wu_buf, wd_buf, sem, x_sem, acc_ref = rest
    i = pl.program_id(0)
    n_chunks, _, tf = wu_hbm.shape
    tm = xo_ref.shape[0]

    def copies(f, slot):
        row0 = pl.multiple_of(f * tf, tf)
        return (pltpu.make_async_copy(wu_hbm.at[f], wu_buf.at[slot], sem.at[0, slot]),
                pltpu.make_async_copy(wd_hbm.at[pl.ds(row0, tf), :], wd_buf.at[slot], sem.at[1, slot]))

    def start(f, slot):
        for cp in copies(f, slot):
            cp.start()

    def wait(f, slot):
        for cp in copies(f, slot):
            cp.wait()

    @pl.when(i == 0)
    def _():
        start(0, 0)

    x_copy = pltpu.make_async_copy(x_hbm.at[pl.ds(pl.multiple_of(i * tm, tm), tm), :], xo_ref, x_sem.at[0])
    x_copy.start()

    def pair(p, carry, first=False):
        for slot in (0, 1):
            f = 2 * p + slot
            start((f + 1) % n_chunks, 1 - slot)
            wait(f, slot)
            a = jnp.square(jnp.maximum(_dot(h_ref[...], wu_buf[slot]), 0.0))
            part = _dot(a.astype(BF16), wd_buf[slot])
            if first and slot == 0:
                acc_ref[...] = part
            else:
                acc_ref[...] += part
        return carry

    pair(0, 0, first=True)
    lax.fori_loop(1, n_chunks // 2, pair, 0)

    x_copy.wait()
    xn = xo_ref[...] + _rms(acc_ref[...], gpost_ref[...])
    xo_ref[...] = xn
    if ho_ref:
        ho_ref[0][...] = _rms(xn, gnext_ref[...]).astype(ho_ref[0].dtype)

    @pl.when(i == n_tiles - 1)
    def _():
        wait(0, 0)


def _ffn(h, w_up, w_down, x, g_post, g_next, layer, next_layer, tm=1024):
    t, d = h.shape
    n = t // tm
    tf = w_up.shape[-1]
    row = pl.BlockSpec((tm, d), lambda i: (i, 0))
    emit_next = next_layer is not None
    out_specs = [row] + ([row] if emit_next else [])
    out_shape = [jax.ShapeDtypeStruct((t, d), F32)] + ([jax.ShapeDtypeStruct((t, d), BF16)] if emit_next else [])
    outs = pl.pallas_call(
        functools.partial(_ffn_kernel, n_tiles=n),
        grid=(n,),
        in_specs=[row, _layer_spec(layer, (1, d)), _layer_spec(next_layer if emit_next else layer, (1, d)),
                  HBM_SPEC, HBM_SPEC, HBM_SPEC],
        out_specs=out_specs,
        out_shape=out_shape,
        scratch_shapes=[pltpu.VMEM((2, d, tf), BF16), pltpu.VMEM((2, tf, d), BF16),
                        pltpu.SemaphoreType.DMA((2, 2)), pltpu.SemaphoreType.DMA((1,)),
                        pltpu.VMEM((tm, d), F32)],
        compiler_params=_params("arbitrary"),
        name="ffn",
    )(h, g_post, g_next, x, w_up, w_down)
    return (outs[0], outs[1]) if emit_next else (outs[0], None)


def kernel(x, mem, norm_mix_pre, norm_mix_post, w_in, w_out, gmlp_v_gain, w_spatial, b_spatial, w_pool,
           s_pool, w_dw, b_dw, conv_ln_g, conv_ln_b, norm_xattn_pre, norm_mem, norm_xattn_post, w_q, w_k,
           w_v, w_o, norm_ffn_pre, norm_ffn_post, w_up, w_down):
    batch, seq, d = x.shape
    mem_len = mem.shape[1]
    depth = w_in.shape[0]
    t = batch * seq

    vec = lambda a: a.reshape(depth, 1, -1)
    norm_mix_pre, norm_mix_post = vec(norm_mix_pre), vec(norm_mix_post)
    norm_xattn_pre, norm_mem, norm_xattn_post = vec(norm_xattn_pre), vec(norm_mem), vec(norm_xattn_post)
    norm_ffn_pre, norm_ffn_post = vec(norm_ffn_pre), vec(norm_ffn_post)
    g_v, s_pool, b_dw = vec(gmlp_v_gain), vec(s_pool), vec(b_dw)
    conv_ln_g, conv_ln_b = vec(conv_ln_g), vec(conv_ln_b)
    b_s_t = jnp.swapaxes(b_spatial, 1, 2)

    xf = x.reshape(t, d)
    memf = mem.reshape(batch * mem_len, d)

    h = _prenorm(xf, norm_mix_pre, 0)
    for l in range(depth):
        u, vn, p, glu, w_down_bf = _mixer_in(h, w_in, g_v, w_down, l)
        y = _mixer_core(u, vn, p, glu, w_spatial, b_s_t, w_pool, s_pool, w_dw, b_dw, conv_ln_g, conv_ln_b,
                        l, seq)
        xf, h = _proj_res(y, w_out, xf, norm_mix_post, norm_xattn_pre, l)
        k, v = _kv_proj(memf, norm_mem, w_k, w_v, l)
        a, w_up_bf = _attention(h, w_q, k, v, w_up, l, seq, mem_len)
        xf, h = _proj_res(a, w_o, xf, norm_xattn_post, norm_ffn_pre, l)
        nxt = l + 1 if l + 1 < depth else None
        xf, h = _ffn(h, w_up_bf, w_down_bf, xf, norm_ffn_post, norm_mix_pre, l, nxt)
    return xf.reshape(batch, seq, d)
```

```python
import functools

import jax
import jax.numpy as jnp
from jax import lax
from jax.experimental import pallas as pl
from jax.experimental.pallas import tpu as pltpu

BF16 = jnp.bfloat16
F32 = jnp.float32

D_MODEL = 2048
HEAD_DIM = 128
CHUNK = 128
GMLP_WIDTH = 1024
GMLP_HEADS = GMLP_WIDTH // HEAD_DIM
POOL_WIDTH = 512
POOL_WINDOWS = (2, 4, 8, 16)
POOL_GROUP_WIDTH = POOL_WIDTH // len(POOL_WINDOWS)
CONV_WIDTH = 512
CONV_K = 31
IN_COLS = 2 * GMLP_WIDTH + POOL_WIDTH + 2 * CONV_WIDTH
XATTN_HEADS = 4
XATTN_HEAD_DIM = D_MODEL // XATTN_HEADS
RMS_EPS = 1e-6
LN_EPS = 1e-5

HALO = 32
SUBLANES = 8
VMEM_LIMIT_BYTES = 56 * 1024 * 1024
PROJ_ROW_CHUNKS = 4
FFN_CHUNK = 512
FFN_FINAL_ROW_BLOCKS = 4
WEIGHT_CHUNK_ROWS = 256


def _params(*semantics):
    return pltpu.CompilerParams(dimension_semantics=semantics, vmem_limit_bytes=VMEM_LIMIT_BYTES)


def _rms(x, g):
    return x * lax.rsqrt(jnp.mean(x * x, axis=-1, keepdims=True) + RMS_EPS) * g


def _layer_norm(x, g, b=None):
    mu = jnp.mean(x, axis=-1, keepdims=True)
    xc = x - mu
    var = jnp.mean(xc * xc, axis=-1, keepdims=True)
    y = xc * lax.rsqrt(var + LN_EPS) * g
    if b is not None:
        y = y + b
    return y


def _dot(a, b):
    return jnp.dot(a, b, preferred_element_type=F32)


def _layer_spec(layer, shape):
    zeros = (0,) * len(shape)
    return pl.BlockSpec((None,) + tuple(shape), lambda *_: (layer,) + zeros)


def _load_cast_weight(w_hbm, layer, wbf_ref, stage_ref, sem):
    chunk_rows = stage_ref.shape[1]
    n_chunks = wbf_ref.shape[0] // chunk_rows

    def copy(c):
        return pltpu.make_async_copy(w_hbm.at[layer, pl.ds(c * chunk_rows, chunk_rows), :],
                                     stage_ref.at[c % 2], sem.at[c % 2])

    copy(0).start()
    for c in range(n_chunks):
        if c + 1 < n_chunks:
            copy(c + 1).start()
        copy(c).wait()
        wbf_ref[c * chunk_rows:(c + 1) * chunk_rows, :] = stage_ref[c % 2].astype(BF16)


def _weight_scratch(k, n, chunk_rows=WEIGHT_CHUNK_ROWS):
    return [pltpu.VMEM((k, n), BF16), pltpu.VMEM((2, chunk_rows, n), F32), pltpu.SemaphoreType.DMA((2,))]


HBM_SPEC = pl.BlockSpec(memory_space=pl.ANY)


def _side_cast_specs(w, layer, n_steps, col_tile=None):
    _, rows, cols = w.shape
    rs = rows // n_steps
    in_spec = pl.BlockSpec((None, rs, cols), lambda i: (layer, i, 0))
    if col_tile is None:
        return in_spec, pl.BlockSpec((rs, cols), lambda i: (i, 0)), jax.ShapeDtypeStruct((rows, cols), BF16)
    n_tiles = cols // col_tile
    out_spec = pl.BlockSpec((n_tiles, rs, col_tile), lambda i: (0, i, 0))
    return in_spec, out_spec, jax.ShapeDtypeStruct((n_tiles, rows, col_tile), BF16)


def _side_cast(side_ref, side_out_ref):
    if len(side_out_ref.shape) == 2:
        side_out_ref[...] = side_ref[...].astype(BF16)
    else:
        n_tiles, _, col_tile = side_out_ref.shape
        for j in range(n_tiles):
            side_out_ref[j] = side_ref[:, j * col_tile:(j + 1) * col_tile].astype(BF16)


def _mixer_in_kernel(h_ref, gpre_ref, gv_ref, w_hbm, side_ref, u_ref, vn_ref, p_ref, glu_ref, side_out_ref,
                     w_ref, stage_ref, sem, *, layer):
    _side_cast(side_ref, side_out_ref)

    @pl.when(pl.program_id(0) == 0)
    def _():
        _load_cast_weight(w_hbm, layer, w_ref, stage_ref, sem)

    h = h_ref[...]
    if h.dtype != BF16:
        h = _rms(h, gpre_ref[...]).astype(BF16)
    u_ref[...] = jax.nn.gelu(_dot(h, w_ref[:, 0:GMLP_WIDTH])).astype(u_ref.dtype)
    v = jax.nn.gelu(_dot(h, w_ref[:, GMLP_WIDTH:2 * GMLP_WIDTH]))
    for hd in range(GMLP_HEADS):
        sl = slice(hd * HEAD_DIM, (hd + 1) * HEAD_DIM)
        vn_ref[:, sl] = _layer_norm(v[:, sl], gv_ref[:, sl]).astype(vn_ref.dtype)
    c0 = 2 * GMLP_WIDTH
    p_ref[...] = _dot(h, w_ref[:, c0:c0 + POOL_WIDTH])
    c1 = c0 + POOL_WIDTH
    c_val = _dot(h, w_ref[:, c1:c1 + CONV_WIDTH])
    c_gate = _dot(h, w_ref[:, c1 + CONV_WIDTH:c1 + 2 * CONV_WIDTH])
    glu_ref[...] = c_val * jax.nn.sigmoid(c_gate)


def _mixer_in(h, g_pre, w_in, g_v, side, layer, tm=512):
    t, d = h.shape
    n = t // tm
    row = lambda c: pl.BlockSpec((tm, c), lambda i: (i, 0))
    side_in, side_out, side_shape = _side_cast_specs(side, layer, n)
    return pl.pallas_call(
        functools.partial(_mixer_in_kernel, layer=layer),
        grid=(n,),
        in_specs=[row(d), _layer_spec(layer, (1, d)), _layer_spec(layer, (1, GMLP_WIDTH)), HBM_SPEC, side_in],
        out_specs=[row(GMLP_WIDTH), row(GMLP_WIDTH), row(POOL_WIDTH), row(CONV_WIDTH), side_out],
        out_shape=[
            jax.ShapeDtypeStruct((t, GMLP_WIDTH), BF16),
            jax.ShapeDtypeStruct((t, GMLP_WIDTH), BF16),
            jax.ShapeDtypeStruct((t, POOL_WIDTH), F32),
            jax.ShapeDtypeStruct((t, CONV_WIDTH), F32),
            side_shape,
        ],
        scratch_shapes=_weight_scratch(d, IN_COLS, chunk_rows=128),
        compiler_params=_params("arbitrary"),
        name="mixer_in",
    )(h, g_pre, g_v, w_in, side)


def _mixer_chunk(c, tile, u_ref, vn_ref, p_ref, ws_ref, bst_ref, wp_ref, sp_ref, bdw_ref, lng_ref, lnb_ref,
                 y_ref, pext_ref, cext_ref, shift_ref, wtap_ref, conv_ref, *, tm, tiles_per_seq, conv_rows):
    row0 = c * CHUNK
    rows = slice(row0, row0 + CHUNK)

    tri = (lax.broadcasted_iota(jnp.int32, (CHUNK, CHUNK), 0)
           >= lax.broadcasted_iota(jnp.int32, (CHUNK, CHUNK), 1))
    for hd in range(GMLP_HEADS):
        w_mask = jnp.where(tri, ws_ref[hd], 0.0).astype(BF16)
        cols = slice(hd * HEAD_DIM, (hd + 1) * HEAD_DIM)
        mixed = _dot(w_mask, vn_ref[rows, cols]) + bst_ref[:, hd:hd + 1]
        y_ref[rows, cols] = (u_ref[rows, cols].astype(F32) * mixed).astype(y_ref.dtype)

    pos = (lax.broadcasted_iota(jnp.int32, (CHUNK, 1), 0) + (tile % tiles_per_seq) * tm + row0).astype(F32)
    for g, w in enumerate(POOL_WINDOWS):
        cols = slice(g * POOL_GROUP_WIDTH, (g + 1) * POOL_GROUP_WIDTH)
        win = pext_ref[HALO + row0:HALO + row0 + CHUNK, cols]
        for k in range(1, w):
            win = win + pext_ref[HALO + row0 - k:HALO + row0 - k + CHUNK, cols]
        pooled = win / jnp.minimum(pos + 1.0, float(w)) - p_ref[rows, cols]
        out = _dot(pooled.astype(BF16), wp_ref[g].astype(BF16)) * sp_ref[:, cols]
        y_ref[rows, GMLP_WIDTH + g * POOL_GROUP_WIDTH:GMLP_WIDTH + (g + 1) * POOL_GROUP_WIDTH] = (
            out.astype(y_ref.dtype))

    n_shift_rows = CHUNK + HALO - SUBLANES
    for s in range(1, SUBLANES):
        shift_ref[s - 1, 0:n_shift_rows, :] = cext_ref[row0 + s:row0 + s + n_shift_rows, :]
    base = HALO - (CONV_K - 1)
    bias = jnp.broadcast_to(bdw_ref[...], (SUBLANES, CONV_WIDTH))
    for r in range(CHUNK // conv_rows):
        acc = [bias] * (conv_rows // SUBLANES)
        for j in range(CONV_K):
            s, start = (base + j) % SUBLANES, r * conv_rows + (base + j) // SUBLANES * SUBLANES
            w_tap = wtap_ref[j]
            for q in range(conv_rows // SUBLANES):
                lo = start + q * SUBLANES
                if s == 0:
                    tap = cext_ref[row0 + lo:row0 + lo + SUBLANES, :]
                else:
                    tap = shift_ref[s - 1, lo:lo + SUBLANES, :]
                acc[q] = acc[q] + w_tap * tap
        for q in range(conv_rows // SUBLANES):
            lo = r * conv_rows + q * SUBLANES
            conv_ref[lo:lo + SUBLANES, :] = acc[q]
    yc = jax.nn.silu(_layer_norm(conv_ref[...], lng_ref[...], lnb_ref[...]))
    y_ref[rows, GMLP_WIDTH + POOL_WIDTH:] = yc.astype(y_ref.dtype)


def _mixer_core_kernel(u_ref, vn_ref, p_ref, ph_ref, c_ref, ch_ref, ws_ref, bst_ref, wp_ref, sp_ref, wdw_ref,
                       bdw_ref, lng_ref, lnb_ref, y_ref, pext_ref, cext_ref, shift_ref, wtap_ref, conv_ref,
                       *, tm, tiles_per_seq):
    tile = pl.program_id(0)
    for j in range(CONV_K):
        wtap_ref[j] = jnp.broadcast_to(wdw_ref[j:j + 1, :], (SUBLANES, CONV_WIDTH))

    keep = jnp.where(tile % tiles_per_seq == 0, 0.0, 1.0).astype(F32)
    pext_ref[0:HALO, :] = ph_ref[...] * keep
    pext_ref[HALO:HALO + tm, :] = p_ref[...]
    cext_ref[0:HALO, :] = ch_ref[...] * keep
    cext_ref[HALO:HALO + tm, :] = c_ref[...]

    for c in range(tm // CHUNK):
        _mixer_chunk(c, tile, u_ref, vn_ref, p_ref, ws_ref, bst_ref, wp_ref, sp_ref, bdw_ref, lng_ref, lnb_ref,
                     y_ref, pext_ref, cext_ref, shift_ref, wtap_ref, conv_ref,
                     tm=tm, tiles_per_seq=tiles_per_seq, conv_rows=32)


def _mixer_core(u, vn, p, glu, w_s, b_s_t, w_pool, s_pool, w_dw, b_dw, ln_g, ln_b, layer, seq, tm=512):
    t = u.shape[0]
    row = lambda n: pl.BlockSpec((tm, n), lambda i: (i, 0))
    halo = lambda n: pl.BlockSpec((HALO, n), lambda i: (jnp.maximum(i * (tm // HALO) - 1, 0), 0))
    kernel = functools.partial(_mixer_core_kernel, tm=tm, tiles_per_seq=seq // tm)
    return pl.pallas_call(
        kernel,
        grid=(t // tm,),
        in_specs=[
            row(GMLP_WIDTH), row(GMLP_WIDTH),
            row(POOL_WIDTH), halo(POOL_WIDTH),
            row(CONV_WIDTH), halo(CONV_WIDTH),
            _layer_spec(layer, (GMLP_HEADS, CHUNK, CHUNK)),
            _layer_spec(layer, (CHUNK, GMLP_HEADS)),
            _layer_spec(layer, (len(POOL_WINDOWS), POOL_GROUP_WIDTH, POOL_GROUP_WIDTH)),
            _layer_spec(layer, (1, POOL_WIDTH)),
            _layer_spec(layer, (CONV_K, CONV_WIDTH)),
            _layer_spec(layer, (1, CONV_WIDTH)),
            _layer_spec(layer, (1, CONV_WIDTH)),
            _layer_spec(layer, (1, CONV_WIDTH)),
        ],
        out_specs=row(D_MODEL),
        out_shape=jax.ShapeDtypeStruct((t, D_MODEL), BF16),
        scratch_shapes=[
            pltpu.VMEM((HALO + tm, POOL_WIDTH), F32),
            pltpu.VMEM((HALO + tm, CONV_WIDTH), F32),
            pltpu.VMEM((SUBLANES - 1, HALO + CHUNK, CONV_WIDTH), F32),
            pltpu.VMEM((CONV_K, SUBLANES, CONV_WIDTH), F32),
            pltpu.VMEM((CHUNK, CONV_WIDTH), F32),
        ],
        compiler_params=_params("parallel"),
        name="mixer_core",
    )(u, vn, p, p, glu, glu, w_s, b_s_t, w_pool, s_pool, w_dw, b_dw, ln_g, ln_b)


def _proj_res_kernel(a_ref, x_ref, gpost_ref, gnext_ref, w_hbm, xo_ref, ho_ref, w_ref, stage_ref, sem, *,
                     layer):
    @pl.when(pl.program_id(0) == 0)
    def _():
        _load_cast_weight(w_hbm, layer, w_ref, stage_ref, sem)

    chunk = a_ref.shape[0] // PROJ_ROW_CHUNKS
    for c in range(PROJ_ROW_CHUNKS):
        rows = slice(c * chunk, (c + 1) * chunk)
        xn = x_ref[rows, :] + _rms(_dot(a_ref[rows, :], w_ref[...]), gpost_ref[...])
        xo_ref[rows, :] = xn
        ho_ref[rows, :] = _rms(xn, gnext_ref[...]).astype(ho_ref.dtype)


def _proj_res(a, w, x, g_post, g_next, layer, tm=512):
    t, k = a.shape
    d = x.shape[1]
    row = lambda c: pl.BlockSpec((tm, c), lambda i: (i, 0))
    return pl.pallas_call(
        functools.partial(_proj_res_kernel, layer=layer),
        grid=(t // tm,),
        in_specs=[row(k), row(d), _layer_spec(layer, (1, d)), _layer_spec(layer, (1, d)), HBM_SPEC],
        out_specs=[row(d), row(d)],
        out_shape=[jax.ShapeDtypeStruct((t, d), F32), jax.ShapeDtypeStruct((t, d), BF16)],
        scratch_shapes=_weight_scratch(k, d),
        compiler_params=_params("arbitrary"),
        name="proj_res",
    )(a, x, g_post, g_next, w)


def _kv_kernel(m_ref, g_ref, wk_ref, wv_ref, k_ref, v_ref):
    m = _rms(m_ref[...], g_ref[...]).astype(BF16)
    k_ref[...] = _dot(m, wk_ref[...].astype(BF16)).astype(k_ref.dtype)
    v_ref[...] = _dot(m, wv_ref[...].astype(BF16)).astype(v_ref.dtype)


def _kv_proj(mem, g_mem, w_k, w_v, layer, tn=256):
    t, d = mem.shape
    full = pl.BlockSpec((t, d), lambda j: (0, 0))
    wcol = pl.BlockSpec((None, d, tn), lambda j: (layer, 0, j))
    ocol = pl.BlockSpec((t, tn), lambda j: (0, j))
    return pl.pallas_call(
        _kv_kernel,
        grid=(d // tn,),
        in_specs=[full, _layer_spec(layer, (1, d)), wcol, wcol],
        out_specs=[ocol, ocol],
        out_shape=[jax.ShapeDtypeStruct((t, d), BF16)] * 2,
        compiler_params=_params("parallel"),
        name="kv_proj",
    )(mem, g_mem, w_k, w_v)


def _attn_kernel(h_ref, k_ref, v_ref, w_hbm, side_ref, o_ref, side_out_ref, wq_ref, stage_ref, sem, *, layer):
    _side_cast(side_ref, side_out_ref)

    @pl.when(pl.program_id(0) == 0)
    def _():
        _load_cast_weight(w_hbm, layer, wq_ref, stage_ref, sem)

    q = _dot(h_ref[...], wq_ref[...]).astype(BF16)
    scale = XATTN_HEAD_DIM ** -0.5
    for hd in range(XATTN_HEADS):
        cols = slice(hd * XATTN_HEAD_DIM, (hd + 1) * XATTN_HEAD_DIM)
        s = lax.dot_general(q[:, cols], k_ref[:, cols], (((1,), (1,)), ((), ())),
                            preferred_element_type=F32) * scale
        e = jnp.exp(s - jnp.max(s, axis=-1, keepdims=True))
        o = _dot(e.astype(BF16), v_ref[:, cols]) / jnp.sum(e, axis=-1, keepdims=True)
        o_ref[:, cols] = o.astype(o_ref.dtype)


def _attention(h, w_q, k, v, side, layer, seq, mem_len, tm=512):
    t, d = h.shape
    n = t // tm
    tiles_per_seq = seq // tm
    row = pl.BlockSpec((tm, d), lambda i: (i, 0))
    mem_rows = pl.BlockSpec((mem_len, d), lambda i: (i // tiles_per_seq, 0))
    side_in, side_out, side_shape = _side_cast_specs(side, layer, n, col_tile=FFN_CHUNK)
    return pl.pallas_call(
        functools.partial(_attn_kernel, layer=layer),
        grid=(n,),
        in_specs=[row, mem_rows, mem_rows, HBM_SPEC, side_in],
        out_specs=[row, side_out],
        out_shape=[jax.ShapeDtypeStruct((t, d), BF16), side_shape],
        scratch_shapes=_weight_scratch(d, d),
        compiler_params=_params("arbitrary"),
        name="attention",
    )(h, k, v, w_q, side)


def _ffn_kernel(h_ref, gpost_ref, gnext_ref, x_hbm, wu_hbm, wd_hbm, xo_ref, *rest, n_tiles):
    *ho_ref, wu_buf, wd_buf, sem, x_sem, acc_ref = rest
    i = pl.program_id(0)
    n_chunks, _, tf = wu_hbm.shape
    tm = xo_ref.shape[0]

    def copies(f, slot):
        row0 = pl.multiple_of(f * tf, tf)
        return (pltpu.make_async_copy(wu_hbm.at[f], wu_buf.at[slot], sem.at[0, slot]),
                pltpu.make_async_copy(wd_hbm.at[pl.ds(row0, tf), :], wd_buf.at[slot], sem.at[1, slot]))

    def start(f, slot):
        for cp in copies(f, slot):
            cp.start()

    def wait(f, slot):
        for cp in copies(f, slot):
            cp.wait()

    @pl.when(i == 0)
    def _():
        start(0, 0)

    x_copy = pltpu.make_async_copy(x_hbm.at[pl.ds(pl.multiple_of(i * tm, tm), tm), :], xo_ref, x_sem.at[0])
    x_copy.start()

    def hidden(rows, slot):
        a = jnp.square(jnp.maximum(_dot(h_ref[rows, :], wu_buf[slot]), 0.0))
        return _dot(a.astype(BF16), wd_buf[slot])

    def finish(rows, acc):
        xn = xo_ref[rows, :] + _rms(acc, gpost_ref[...])
        xo_ref[rows, :] = xn
        if ho_ref:
            ho_ref[0][rows, :] = _rms(xn, gnext_ref[...]).astype(ho_ref[0].dtype)

    def pair(p, carry, first=False, last=False):
        for slot in (0, 1):
            f = 2 * p + slot
            start((f + 1) % n_chunks, 1 - slot)
            wait(f, slot)
            if last and slot == 1:
                x_copy.wait()
                block = tm // FFN_FINAL_ROW_BLOCKS
                for r in range(FFN_FINAL_ROW_BLOCKS):
                    rows = slice(r * block, (r + 1) * block)
                    finish(rows, acc_ref[rows, :] + hidden(rows, slot))
            elif first and slot == 0:
                acc_ref[...] = hidden(slice(None), slot)
            else:
                acc_ref[...] += hidden(slice(None), slot)
        return carry

    n_pairs = n_chunks // 2
    pair(0, 0, first=True)
    lax.fori_loop(1, n_pairs - 1, pair, 0)
    pair(n_pairs - 1, 0, last=True)

    @pl.when(i == n_tiles - 1)
    def _():
        wait(0, 0)


def _ffn(h, w_up, w_down, x, g_post, g_next, layer, next_layer, tm=1024):
    t, d = h.shape
    n = t // tm
    tf = w_up.shape[-1]
    row = pl.BlockSpec((tm, d), lambda i: (i, 0))
    emit_next = next_layer is not None
    out_specs = [row] + ([row] if emit_next else [])
    out_shape = [jax.ShapeDtypeStruct((t, d), F32)] + ([jax.ShapeDtypeStruct((t, d), BF16)] if emit_next else [])
    outs = pl.pallas_call(
        functools.partial(_ffn_kernel, n_tiles=n),
        grid=(n,),
        in_specs=[row, _layer_spec(layer, (1, d)), _layer_spec(next_layer if emit_next else layer, (1, d)),
                  HBM_SPEC, HBM_SPEC, HBM_SPEC],
        out_specs=out_specs,
        out_shape=out_shape,
        scratch_shapes=[pltpu.VMEM((2, d, tf), BF16), pltpu.VMEM((2, tf, d), BF16),
                        pltpu.SemaphoreType.DMA((2, 2)), pltpu.SemaphoreType.DMA((1,)),
                        pltpu.VMEM((tm, d), F32)],
        compiler_params=_params("arbitrary"),
        name="ffn",
    )(h, g_post, g_next, x, w_up, w_down)
    return (outs[0], outs[1]) if emit_next else (outs[0], None)


def kernel(x, mem, norm_mix_pre, norm_mix_post, w_in, w_out, gmlp_v_gain, w_spatial, b_spatial, w_pool,
           s_pool, w_dw, b_dw, conv_ln_g, conv_ln_b, norm_xattn_pre, norm_mem, norm_xattn_post, w_q, w_k,
           w_v, w_o, norm_ffn_pre, norm_ffn_post, w_up, w_down):
    batch, seq, d = x.shape
    mem_len = mem.shape[1]
    depth = w_in.shape[0]
    t = batch * seq

    vec = lambda a: a.reshape(depth, 1, -1)
    norm_mix_pre, norm_mix_post = vec(norm_mix_pre), vec(norm_mix_post)
    norm_xattn_pre, norm_mem, norm_xattn_post = vec(norm_xattn_pre), vec(norm_mem), vec(norm_xattn_post)
    norm_ffn_pre, norm_ffn_post = vec(norm_ffn_pre), vec(norm_ffn_post)
    g_v, s_pool, b_dw = vec(gmlp_v_gain), vec(s_pool), vec(b_dw)
    conv_ln_g, conv_ln_b = vec(conv_ln_g), vec(conv_ln_b)
    b_s_t = jnp.swapaxes(b_spatial, 1, 2)

    xf = x.reshape(t, d)
    memf = mem.reshape(batch * mem_len, d)

    h = xf
    for l in range(depth):
        u, vn, p, glu, w_down_bf = _mixer_in(h, norm_mix_pre, w_in, g_v, w_down, l)
        y = _mixer_core(u, vn, p, glu, w_spatial, b_s_t, w_pool, s_pool, w_dw, b_dw, conv_ln_g, conv_ln_b,
                        l, seq)
        xf, h = _proj_res(y, w_out, xf, norm_mix_post, norm_xattn_pre, l)
        k, v = _kv_proj(memf, norm_mem, w_k, w_v, l)
        a, w_up_bf = _attention(h, w_q, k, v, w_up, l, seq, mem_len)
        xf, h = _proj_res(a, w_o, xf, norm_xattn_post, norm_ffn_pre, l)
        nxt = l + 1 if l + 1 < depth else None
        xf, h = _ffn(h, w_up_bf, w_down_bf, xf, norm_ffn_post, norm_mix_pre, l, nxt)
    return xf.reshape(batch, seq, d)
```

```python
import functools

import jax
import jax.numpy as jnp
from jax import lax
from jax.experimental import pallas as pl
from jax.experimental.pallas import tpu as pltpu

BF16 = jnp.bfloat16
F32 = jnp.float32

D_MODEL = 2048
HEAD_DIM = 128
CHUNK = 128
GMLP_WIDTH = 1024
GMLP_HEADS = GMLP_WIDTH // HEAD_DIM
POOL_WIDTH = 512
POOL_WINDOWS = (2, 4, 8, 16)
POOL_GROUP_WIDTH = POOL_WIDTH // len(POOL_WINDOWS)
CONV_WIDTH = 512
CONV_K = 31
IN_COLS = 2 * GMLP_WIDTH + POOL_WIDTH + 2 * CONV_WIDTH
XATTN_HEADS = 4
XATTN_HEAD_DIM = D_MODEL // XATTN_HEADS
RMS_EPS = 1e-6
LN_EPS = 1e-5

HALO = 32
SUBLANES = 8
VMEM_LIMIT_BYTES = 56 * 1024 * 1024
PROJ_ROW_CHUNKS = 4
FFN_CHUNK = 512
FFN_FINAL_ROW_BLOCKS = 4
WEIGHT_CHUNK_ROWS = 256


def _params(*semantics):
    return pltpu.CompilerParams(dimension_semantics=semantics, vmem_limit_bytes=VMEM_LIMIT_BYTES)


def _rms(x, g):
    return x * lax.rsqrt(jnp.mean(x * x, axis=-1, keepdims=True) + RMS_EPS) * g


def _layer_norm(x, g, b=None):
    mu = jnp.mean(x, axis=-1, keepdims=True)
    xc = x - mu
    var = jnp.mean(xc * xc, axis=-1, keepdims=True)
    y = xc * lax.rsqrt(var + LN_EPS) * g
    if b is not None:
        y = y + b
    return y


def _dot(a, b):
    return jnp.dot(a, b, preferred_element_type=F32)


def _layer_spec(layer, shape):
    zeros = (0,) * len(shape)
    return pl.BlockSpec((None,) + tuple(shape), lambda *_: (layer,) + zeros)


def _load_cast_weight(w_hbm, layer, wbf_ref, stage_ref, sem):
    chunk_rows = stage_ref.shape[1]
    n_chunks = wbf_ref.shape[0] // chunk_rows

    def copy(c):
        return pltpu.make_async_copy(w_hbm.at[layer, pl.ds(c * chunk_rows, chunk_rows), :],
                                     stage_ref.at[c % 2], sem.at[c % 2])

    copy(0).start()
    for c in range(n_chunks):
        if c + 1 < n_chunks:
            copy(c + 1).start()
        copy(c).wait()
        wbf_ref[c * chunk_rows:(c + 1) * chunk_rows, :] = stage_ref[c % 2].astype(BF16)


def _load_cast_matmul(a_ref, w_hbm, layer, wbf_ref, stage_ref, sem):
    chunk_rows = stage_ref.shape[1]
    n_chunks = wbf_ref.shape[0] // chunk_rows

    def copy(c):
        return pltpu.make_async_copy(w_hbm.at[layer, pl.ds(c * chunk_rows, chunk_rows), :],
                                     stage_ref.at[c % 2], sem.at[c % 2])

    copy(0).start()
    acc = None
    for c in range(n_chunks):
        if c + 1 < n_chunks:
            copy(c + 1).start()
        copy(c).wait()
        rows = slice(c * chunk_rows, (c + 1) * chunk_rows)
        wbf_ref[rows, :] = stage_ref[c % 2].astype(BF16)
        part = _dot(a_ref[:, rows], wbf_ref[rows, :])
        acc = part if acc is None else acc + part
    return acc


def _weight_scratch(k, n, chunk_rows=WEIGHT_CHUNK_ROWS):
    return [pltpu.VMEM((k, n), BF16), pltpu.VMEM((2, chunk_rows, n), F32), pltpu.SemaphoreType.DMA((2,))]


HBM_SPEC = pl.BlockSpec(memory_space=pl.ANY)


def _side_cast_specs(w, layer, n_steps, col_tile=None):
    _, rows, cols = w.shape
    rs = rows // n_steps
    in_spec = pl.BlockSpec((None, rs, cols), lambda i: (layer, i, 0))
    if col_tile is None:
        return in_spec, pl.BlockSpec((rs, cols), lambda i: (i, 0)), jax.ShapeDtypeStruct((rows, cols), BF16)
    n_tiles = cols // col_tile
    out_spec = pl.BlockSpec((n_tiles, rs, col_tile), lambda i: (0, i, 0))
    return in_spec, out_spec, jax.ShapeDtypeStruct((n_tiles, rows, col_tile), BF16)


def _side_cast(side_ref, side_out_ref):
    if len(side_out_ref.shape) == 2:
        side_out_ref[...] = side_ref[...].astype(BF16)
    else:
        n_tiles, _, col_tile = side_out_ref.shape
        for j in range(n_tiles):
            side_out_ref[j] = side_ref[:, j * col_tile:(j + 1) * col_tile].astype(BF16)


def _mixer_in_kernel(h_ref, gpre_ref, gv_ref, w_hbm, side_ref, u_ref, vn_ref, p_ref, glu_ref, side_out_ref,
                     w_ref, stage_ref, sem, *, layer):
    _side_cast(side_ref, side_out_ref)

    @pl.when(pl.program_id(0) == 0)
    def _():
        _load_cast_weight(w_hbm, layer, w_ref, stage_ref, sem)

    h = h_ref[...]
    if h.dtype != BF16:
        h = _rms(h, gpre_ref[...]).astype(BF16)
    u_ref[...] = jax.nn.gelu(_dot(h, w_ref[:, 0:GMLP_WIDTH])).astype(u_ref.dtype)
    v = jax.nn.gelu(_dot(h, w_ref[:, GMLP_WIDTH:2 * GMLP_WIDTH]))
    for hd in range(GMLP_HEADS):
        sl = slice(hd * HEAD_DIM, (hd + 1) * HEAD_DIM)
        vn_ref[:, sl] = _layer_norm(v[:, sl], gv_ref[:, sl]).astype(vn_ref.dtype)
    c0 = 2 * GMLP_WIDTH
    p_ref[...] = _dot(h, w_ref[:, c0:c0 + POOL_WIDTH])
    c1 = c0 + POOL_WIDTH
    c_val = _dot(h, w_ref[:, c1:c1 + CONV_WIDTH])
    c_gate = _dot(h, w_ref[:, c1 + CONV_WIDTH:c1 + 2 * CONV_WIDTH])
    glu_ref[...] = c_val * jax.nn.sigmoid(c_gate)


def _mixer_in(h, g_pre, w_in, g_v, side, layer, tm=512):
    t, d = h.shape
    n = t // tm
    row = lambda c: pl.BlockSpec((tm, c), lambda i: (i, 0))
    side_in, side_out, side_shape = _side_cast_specs(side, layer, n)
    return pl.pallas_call(
        functools.partial(_mixer_in_kernel, layer=layer),
        grid=(n,),
        in_specs=[row(d), _layer_spec(layer, (1, d)), _layer_spec(layer, (1, GMLP_WIDTH)), HBM_SPEC, side_in],
        out_specs=[row(GMLP_WIDTH), row(GMLP_WIDTH), row(POOL_WIDTH), row(CONV_WIDTH), side_out],
        out_shape=[
            jax.ShapeDtypeStruct((t, GMLP_WIDTH), BF16),
            jax.ShapeDtypeStruct((t, GMLP_WIDTH), BF16),
            jax.ShapeDtypeStruct((t, POOL_WIDTH), F32),
            jax.ShapeDtypeStruct((t, CONV_WIDTH), F32),
            side_shape,
        ],
        scratch_shapes=_weight_scratch(d, IN_COLS, chunk_rows=128),
        compiler_params=_params("arbitrary"),
        name="mixer_in",
    )(h, g_pre, g_v, w_in, side)


def _mixer_chunk(c, tile, u_ref, vn_ref, p_ref, ws_ref, bst_ref, wp_ref, sp_ref, bdw_ref, lng_ref, lnb_ref,
                 y_ref, pext_ref, cext_ref, shift_ref, wtap_ref, conv_ref, *, tm, tiles_per_seq, conv_rows):
    row0 = c * CHUNK
    rows = slice(row0, row0 + CHUNK)

    tri = (lax.broadcasted_iota(jnp.int32, (CHUNK, CHUNK), 0)
           >= lax.broadcasted_iota(jnp.int32, (CHUNK, CHUNK), 1))
    for hd in range(GMLP_HEADS):
        w_mask = jnp.where(tri, ws_ref[hd], 0.0).astype(BF16)
        cols = slice(hd * HEAD_DIM, (hd + 1) * HEAD_DIM)
        mixed = _dot(w_mask, vn_ref[rows, cols]) + bst_ref[:, hd:hd + 1]
        y_ref[rows, cols] = (u_ref[rows, cols].astype(F32) * mixed).astype(y_ref.dtype)

    pos = (lax.broadcasted_iota(jnp.int32, (CHUNK, 1), 0) + (tile % tiles_per_seq) * tm + row0).astype(F32)
    for g, w in enumerate(POOL_WINDOWS):
        cols = slice(g * POOL_GROUP_WIDTH, (g + 1) * POOL_GROUP_WIDTH)
        win = pext_ref[HALO + row0 - (w - 1):HALO + row0 + CHUNK, cols]
        span = 1
        while span < w:
            win = win[span:, :] + win[:-span, :]
            span *= 2
        pooled = win / jnp.minimum(pos + 1.0, float(w)) - p_ref[rows, cols]
        out = _dot(pooled.astype(BF16), wp_ref[g].astype(BF16)) * sp_ref[:, cols]
        y_ref[rows, GMLP_WIDTH + g * POOL_GROUP_WIDTH:GMLP_WIDTH + (g + 1) * POOL_GROUP_WIDTH] = (
            out.astype(y_ref.dtype))

    n_shift_rows = CHUNK + HALO - SUBLANES
    for s in range(1, SUBLANES):
        shift_ref[s - 1, 0:n_shift_rows, :] = cext_ref[row0 + s:row0 + s + n_shift_rows, :]
    base = HALO - (CONV_K - 1)
    bias = jnp.broadcast_to(bdw_ref[...], (SUBLANES, CONV_WIDTH))
    for r in range(CHUNK // conv_rows):
        acc = [bias] * (conv_rows // SUBLANES)
        for j in range(CONV_K):
            s, start = (base + j) % SUBLANES, r * conv_rows + (base + j) // SUBLANES * SUBLANES
            w_tap = wtap_ref[j]
            for q in range(conv_rows // SUBLANES):
                lo = start + q * SUBLANES
                if s == 0:
                    tap = cext_ref[row0 + lo:row0 + lo + SUBLANES, :]
                else:
                    tap = shift_ref[s - 1, lo:lo + SUBLANES, :]
                acc[q] = acc[q] + w_tap * tap
        for q in range(conv_rows // SUBLANES):
            lo = r * conv_rows + q * SUBLANES
            conv_ref[lo:lo + SUBLANES, :] = acc[q]
    yc = jax.nn.silu(_layer_norm(conv_ref[...], lng_ref[...], lnb_ref[...]))
    y_ref[rows, GMLP_WIDTH + POOL_WIDTH:] = yc.astype(y_ref.dtype)


def _mixer_core_kernel(u_ref, vn_ref, p_ref, ph_ref, c_ref, ch_ref, ws_ref, bst_ref, wp_ref, sp_ref, wdw_ref,
                       bdw_ref, lng_ref, lnb_ref, y_ref, pext_ref, cext_ref, shift_ref, wtap_ref, conv_ref,
                       *, tm, tiles_per_seq):
    tile = pl.program_id(0)
    for j in range(CONV_K):
        wtap_ref[j] = jnp.broadcast_to(wdw_ref[j:j + 1, :], (SUBLANES, CONV_WIDTH))

    keep = jnp.where(tile % tiles_per_seq == 0, 0.0, 1.0).astype(F32)
    pext_ref[0:HALO, :] = ph_ref[...] * keep
    pext_ref[HALO:HALO + tm, :] = p_ref[...]
    cext_ref[0:HALO, :] = ch_ref[...] * keep
    cext_ref[HALO:HALO + tm, :] = c_ref[...]

    for c in range(tm // CHUNK):
        _mixer_chunk(c, tile, u_ref, vn_ref, p_ref, ws_ref, bst_ref, wp_ref, sp_ref, bdw_ref, lng_ref, lnb_ref,
                     y_ref, pext_ref, cext_ref, shift_ref, wtap_ref, conv_ref,
                     tm=tm, tiles_per_seq=tiles_per_seq, conv_rows=32)


def _mixer_core(u, vn, p, glu, w_s, b_s_t, w_pool, s_pool, w_dw, b_dw, ln_g, ln_b, layer, seq, tm=512):
    t = u.shape[0]
    row = lambda n: pl.BlockSpec((tm, n), lambda i: (i, 0))
    halo = lambda n: pl.BlockSpec((HALO, n), lambda i: (jnp.maximum(i * (tm // HALO) - 1, 0), 0))
    kernel = functools.partial(_mixer_core_kernel, tm=tm, tiles_per_seq=seq // tm)
    return pl.pallas_call(
        kernel,
        grid=(t // tm,),
        in_specs=[
            row(GMLP_WIDTH), row(GMLP_WIDTH),
            row(POOL_WIDTH), halo(POOL_WIDTH),
            row(CONV_WIDTH), halo(CONV_WIDTH),
            _layer_spec(layer, (GMLP_HEADS, CHUNK, CHUNK)),
            _layer_spec(layer, (CHUNK, GMLP_HEADS)),
            _layer_spec(layer, (len(POOL_WINDOWS), POOL_GROUP_WIDTH, POOL_GROUP_WIDTH)),
            _layer_spec(layer, (1, POOL_WIDTH)),
            _layer_spec(layer, (CONV_K, CONV_WIDTH)),
            _layer_spec(layer, (1, CONV_WIDTH)),
            _layer_spec(layer, (1, CONV_WIDTH)),
            _layer_spec(layer, (1, CONV_WIDTH)),
        ],
        out_specs=row(D_MODEL),
        out_shape=jax.ShapeDtypeStruct((t, D_MODEL), BF16),
        scratch_shapes=[
            pltpu.VMEM((HALO + tm, POOL_WIDTH), F32),
            pltpu.VMEM((HALO + tm, CONV_WIDTH), F32),
            pltpu.VMEM((SUBLANES - 1, HALO + CHUNK, CONV_WIDTH), F32),
            pltpu.VMEM((CONV_K, SUBLANES, CONV_WIDTH), F32),
            pltpu.VMEM((CHUNK, CONV_WIDTH), F32),
        ],
        compiler_params=_params("parallel"),
        name="mixer_core",
    )(u, vn, p, p, glu, glu, w_s, b_s_t, w_pool, s_pool, w_dw, b_dw, ln_g, ln_b)


def _proj_res_kernel(a_ref, x_ref, gpost_ref, gnext_ref, w_hbm, xo_ref, ho_ref, w_ref, stage_ref, sem, *,
                     layer):
    def finish(rows, h):
        xn = x_ref[rows, :] + _rms(h, gpost_ref[...])
        xo_ref[rows, :] = xn
        ho_ref[rows, :] = _rms(xn, gnext_ref[...]).astype(ho_ref.dtype)

    @pl.when(pl.program_id(0) == 0)
    def _():
        finish(slice(None), _load_cast_matmul(a_ref, w_hbm, layer, w_ref, stage_ref, sem))

    @pl.when(pl.program_id(0) > 0)
    def _():
        chunk = a_ref.shape[0] // PROJ_ROW_CHUNKS
        for c in range(PROJ_ROW_CHUNKS):
            rows = slice(c * chunk, (c + 1) * chunk)
            finish(rows, _dot(a_ref[rows, :], w_ref[...]))


def _proj_res(a, w, x, g_post, g_next, layer, tm=512):
    t, k = a.shape
    d = x.shape[1]
    row = lambda c: pl.BlockSpec((tm, c), lambda i: (i, 0))
    return pl.pallas_call(
        functools.partial(_proj_res_kernel, layer=layer),
        grid=(t // tm,),
        in_specs=[row(k), row(d), _layer_spec(layer, (1, d)), _layer_spec(layer, (1, d)), HBM_SPEC],
        out_specs=[row(d), row(d)],
        out_shape=[jax.ShapeDtypeStruct((t, d), F32), jax.ShapeDtypeStruct((t, d), BF16)],
        scratch_shapes=_weight_scratch(k, d),
        compiler_params=_params("arbitrary"),
        name="proj_res",
    )(a, x, g_post, g_next, w)


def _kv_kernel(m_ref, g_ref, wk_ref, wv_ref, k_ref, v_ref):
    m = _rms(m_ref[...], g_ref[...]).astype(BF16)
    k_ref[...] = _dot(m, wk_ref[...].astype(BF16)).astype(k_ref.dtype)
    v_ref[...] = _dot(m, wv_ref[...].astype(BF16)).astype(v_ref.dtype)


def _kv_proj(mem, g_mem, w_k, w_v, layer, tn=256):
    t, d = mem.shape
    full = pl.BlockSpec((t, d), lambda j: (0, 0))
    wcol = pl.BlockSpec((None, d, tn), lambda j: (layer, 0, j))
    ocol = pl.BlockSpec((t, tn), lambda j: (0, j))
    return pl.pallas_call(
        _kv_kernel,
        grid=(d // tn,),
        in_specs=[full, _layer_spec(layer, (1, d)), wcol, wcol],
        out_specs=[ocol, ocol],
        out_shape=[jax.ShapeDtypeStruct((t, d), BF16)] * 2,
        compiler_params=_params("parallel"),
        name="kv_proj",
    )(mem, g_mem, w_k, w_v)


def _attn_kernel(h_ref, k_ref, v_ref, w_hbm, side_ref, o_ref, side_out_ref, wq_ref, stage_ref, sem, *, layer):
    _side_cast(side_ref, side_out_ref)

    def attend(q):
        q = q.astype(BF16)
        scale = XATTN_HEAD_DIM ** -0.5
        for hd in range(XATTN_HEADS):
            cols = slice(hd * XATTN_HEAD_DIM, (hd + 1) * XATTN_HEAD_DIM)
            s = lax.dot_general(q[:, cols], k_ref[:, cols], (((1,), (1,)), ((), ())),
                                preferred_element_type=F32) * scale
            e = jnp.exp(s - jnp.max(s, axis=-1, keepdims=True))
            o = _dot(e.astype(BF16), v_ref[:, cols]) / jnp.sum(e, axis=-1, keepdims=True)
            o_ref[:, cols] = o.astype(o_ref.dtype)

    @pl.when(pl.program_id(0) == 0)
    def _():
        attend(_load_cast_matmul(h_ref, w_hbm, layer, wq_ref, stage_ref, sem))

    @pl.when(pl.program_id(0) > 0)
    def _():
        attend(_dot(h_ref[...], wq_ref[...]))


def _attention(h, w_q, k, v, side, layer, seq, mem_len, tm=512):
    t, d = h.shape
    n = t // tm
    tiles_per_seq = seq // tm
    row = pl.BlockSpec((tm, d), lambda i: (i, 0))
    mem_rows = pl.BlockSpec((mem_len, d), lambda i: (i // tiles_per_seq, 0))
    side_in, side_out, side_shape = _side_cast_specs(side, layer, n, col_tile=FFN_CHUNK)
    return pl.pallas_call(
        functools.partial(_attn_kernel, layer=layer),
        grid=(n,),
        in_specs=[row, mem_rows, mem_rows, HBM_SPEC, side_in],
        out_specs=[row, side_out],
        out_shape=[jax.ShapeDtypeStruct((t, d), BF16), side_shape],
        scratch_shapes=_weight_scratch(d, d),
        compiler_params=_params("arbitrary"),
        name="attention",
    )(h, k, v, w_q, side)


def _ffn_kernel(h_ref, gpost_ref, gnext_ref, x_hbm, wu_hbm, wd_hbm, xo_ref, *rest, n_tiles):
    *ho_ref, wu_buf, wd_buf, sem, x_sem, acc_ref = rest
    i = pl.program_id(0)
    n_chunks, _, tf = wu_hbm.shape
    tm = xo_ref.shape[0]

    def copies(f, slot):
        row0 = pl.multiple_of(f * tf, tf)
        return (pltpu.make_async_copy(wu_hbm.at[f], wu_buf.at[slot], sem.at[0, slot]),
                pltpu.make_async_copy(wd_hbm.at[pl.ds(row0, tf), :], wd_buf.at[slot], sem.at[1, slot]))

    def start(f, slot):
        for cp in copies(f, slot):
            cp.start()

    def wait(f, slot):
        for cp in copies(f, slot):
            cp.wait()

    @pl.when(i == 0)
    def _():
        start(0, 0)

    x_copy = pltpu.make_async_copy(x_hbm.at[pl.ds(pl.multiple_of(i * tm, tm), tm), :], xo_ref, x_sem.at[0])
    x_copy.start()

    def hidden(rows, slot):
        a = jnp.square(jnp.maximum(_dot(h_ref[rows, :], wu_buf[slot]), 0.0))
        return _dot(a.astype(BF16), wd_buf[slot])

    def finish(rows, acc):
        xn = xo_ref[rows, :] + _rms(acc, gpost_ref[...])
        xo_ref[rows, :] = xn
        if ho_ref:
            ho_ref[0][rows, :] = _rms(xn, gnext_ref[...]).astype(ho_ref[0].dtype)

    def pair(p, carry, first=False, last=False):
        for slot in (0, 1):
            f = 2 * p + slot
            start((f + 1) % n_chunks, 1 - slot)
            wait(f, slot)
            if last and slot == 1:
                x_copy.wait()
                block = tm // FFN_FINAL_ROW_BLOCKS
                for r in range(FFN_FINAL_ROW_BLOCKS):
                    rows = slice(r * block, (r + 1) * block)
                    finish(rows, acc_ref[rows, :] + hidden(rows, slot))
            elif first and slot == 0:
                acc_ref[...] = hidden(slice(None), slot)
            else:
                acc_ref[...] += hidden(slice(None), slot)
        return carry

    n_pairs = n_chunks // 2
    pair(0, 0, first=True)
    lax.fori_loop(1, n_pairs - 1, pair, 0)
    pair(n_pairs - 1, 0, last=True)

    @pl.when(i == n_tiles - 1)
    def _():
        wait(0, 0)


def _ffn(h, w_up, w_down, x, g_post, g_next, layer, next_layer, tm=1024):
    t, d = h.shape
    n = t // tm
    tf = w_up.shape[-1]
    row = pl.BlockSpec((tm, d), lambda i: (i, 0))
    emit_next = next_layer is not None
    out_specs = [row] + ([row] if emit_next else [])
    out_shape = [jax.ShapeDtypeStruct((t, d), F32)] + ([jax.ShapeDtypeStruct((t, d), BF16)] if emit_next else [])
    outs = pl.pallas_call(
        functools.partial(_ffn_kernel, n_tiles=n),
        grid=(n,),
        in_specs=[row, _layer_spec(layer, (1, d)), _layer_spec(next_layer if emit_next else layer, (1, d)),
                  HBM_SPEC, HBM_SPEC, HBM_SPEC],
        out_specs=out_specs,
        out_shape=out_shape,
        scratch_shapes=[pltpu.VMEM((2, d, tf), BF16), pltpu.VMEM((2, tf, d), BF16),
                        pltpu.SemaphoreType.DMA((2, 2)), pltpu.SemaphoreType.DMA((1,)),
                        pltpu.VMEM((tm, d), F32)],
        compiler_params=_params("arbitrary"),
        name="ffn",
    )(h, g_post, g_next, x, w_up, w_down)
    return (outs[0], outs[1]) if emit_next else (outs[0], None)


def kernel(x, mem, norm_mix_pre, norm_mix_post, w_in, w_out, gmlp_v_gain, w_spatial, b_spatial, w_pool,
           s_pool, w_dw, b_dw, conv_ln_g, conv_ln_b, norm_xattn_pre, norm_mem, norm_xattn_post, w_q, w_k,
           w_v, w_o, norm_ffn_pre, norm_ffn_post, w_up, w_down):
    batch, seq, d = x.shape
    mem_len = mem.shape[1]
    depth = w_in.shape[0]
    t = batch * seq

    vec = lambda a: a.reshape(depth, 1, -1)
    norm_mix_pre, norm_mix_post = vec(norm_mix_pre), vec(norm_mix_post)
    norm_xattn_pre, norm_mem, norm_xattn_post = vec(norm_xattn_pre), vec(norm_mem), vec(norm_xattn_post)
    norm_ffn_pre, norm_ffn_post = vec(norm_ffn_pre), vec(norm_ffn_post)
    g_v, s_pool, b_dw = vec(gmlp_v_gain), vec(s_pool), vec(b_dw)
    conv_ln_g, conv_ln_b = vec(conv_ln_g), vec(conv_ln_b)
    b_s_t = jnp.swapaxes(b_spatial, 1, 2)

    xf = x.reshape(t, d)
    memf = mem.reshape(batch * mem_len, d)

    h = xf
    for l in range(depth):
        u, vn, p, glu, w_down_bf = _mixer_in(h, norm_mix_pre, w_in, g_v, w_down, l)
        y = _mixer_core(u, vn, p, glu, w_spatial, b_s_t, w_pool, s_pool, w_dw, b_dw, conv_ln_g, conv_ln_b,
                        l, seq)
        xf, h = _proj_res(y, w_out, xf, norm_mix_post, norm_xattn_pre, l)
        k, v = _kv_proj(memf, norm_mem, w_k, w_v, l)
        a, w_up_bf = _attention(h, w_q, k, v, w_up, l, seq, mem_len)
        xf, h = _proj_res(a, w_o, xf, norm_xattn_post, norm_ffn_pre, l)
        nxt = l + 1 if l + 1 < depth else None
        xf, h = _ffn(h, w_up_bf, w_down_bf, xf, norm_ffn_post, norm_mix_pre, l, nxt)
    return xf.reshape(batch, seq, d)
```

```python
import functools

import jax
import jax.numpy as jnp
from jax import lax
from jax.experimental import pallas as pl
from jax.experimental.pallas import tpu as pltpu

BF16 = jnp.bfloat16
F32 = jnp.float32

D_MODEL = 2048
HEAD_DIM = 128
CHUNK = 128
GMLP_WIDTH = 1024
GMLP_HEADS = GMLP_WIDTH // HEAD_DIM
POOL_WIDTH = 512
POOL_WINDOWS = (2, 4, 8, 16)
POOL_GROUP_WIDTH = POOL_WIDTH // len(POOL_WINDOWS)
CONV_WIDTH = 512
CONV_K = 31
IN_COLS = 2 * GMLP_WIDTH + POOL_WIDTH + 2 * CONV_WIDTH
XATTN_HEADS = 4
XATTN_HEAD_DIM = D_MODEL // XATTN_HEADS
RMS_EPS = 1e-6
LN_EPS = 1e-5

HALO = 32
SUBLANES = 8
VMEM_LIMIT_BYTES = 56 * 1024 * 1024
PROJ_ROW_CHUNKS = 4
FFN_CHUNK = 512
FFN_FINAL_ROW_BLOCKS = 4
WEIGHT_CHUNK_ROWS = 256


def _params(*semantics):
    return pltpu.CompilerParams(dimension_semantics=semantics, vmem_limit_bytes=VMEM_LIMIT_BYTES)


def _rms(x, g):
    return x * lax.rsqrt(jnp.mean(x * x, axis=-1, keepdims=True) + RMS_EPS) * g


def _layer_norm(x, g, b=None):
    mu = jnp.mean(x, axis=-1, keepdims=True)
    xc = x - mu
    var = jnp.mean(xc * xc, axis=-1, keepdims=True)
    y = xc * lax.rsqrt(var + LN_EPS) * g
    if b is not None:
        y = y + b
    return y


def _dot(a, b):
    return jnp.dot(a, b, preferred_element_type=F32)


def _layer_spec(layer, shape):
    zeros = (0,) * len(shape)
    return pl.BlockSpec((None,) + tuple(shape), lambda *_: (layer,) + zeros)


def _load_cast_weight(w_hbm, layer, wbf_ref, stage_ref, sem):
    chunk_rows = stage_ref.shape[1]
    n_chunks = wbf_ref.shape[0] // chunk_rows

    def copy(c):
        return pltpu.make_async_copy(w_hbm.at[layer, pl.ds(c * chunk_rows, chunk_rows), :],
                                     stage_ref.at[c % 2], sem.at[c % 2])

    copy(0).start()
    for c in range(n_chunks):
        if c + 1 < n_chunks:
            copy(c + 1).start()
        copy(c).wait()
        wbf_ref[c * chunk_rows:(c + 1) * chunk_rows, :] = stage_ref[c % 2].astype(BF16)


def _load_cast_matmul(a_ref, w_hbm, layer, wbf_ref, stage_ref, sem):
    chunk_rows = stage_ref.shape[1]
    n_chunks = wbf_ref.shape[0] // chunk_rows

    def copy(c):
        return pltpu.make_async_copy(w_hbm.at[layer, pl.ds(c * chunk_rows, chunk_rows), :],
                                     stage_ref.at[c % 2], sem.at[c % 2])

    copy(0).start()
    acc = None
    for c in range(n_chunks):
        if c + 1 < n_chunks:
            copy(c + 1).start()
        copy(c).wait()
        rows = slice(c * chunk_rows, (c + 1) * chunk_rows)
        wbf_ref[rows, :] = stage_ref[c % 2].astype(BF16)
        part = _dot(a_ref[:, rows], wbf_ref[rows, :])
        acc = part if acc is None else acc + part
    return acc


def _weight_scratch(k, n, chunk_rows=WEIGHT_CHUNK_ROWS):
    return [pltpu.VMEM((k, n), BF16), pltpu.VMEM((2, chunk_rows, n), F32), pltpu.SemaphoreType.DMA((2,))]


HBM_SPEC = pl.BlockSpec(memory_space=pl.ANY)


def _side_cast_specs(w, layer, n_steps, col_tile=None):
    _, rows, cols = w.shape
    rs = rows // n_steps
    in_spec = pl.BlockSpec((None, rs, cols), lambda i: (layer, i, 0))
    if col_tile is None:
        return in_spec, pl.BlockSpec((rs, cols), lambda i: (i, 0)), jax.ShapeDtypeStruct((rows, cols), BF16)
    n_tiles = cols // col_tile
    out_spec = pl.BlockSpec((n_tiles, rs, col_tile), lambda i: (0, i, 0))
    return in_spec, out_spec, jax.ShapeDtypeStruct((n_tiles, rows, col_tile), BF16)


def _side_cast(side_ref, side_out_ref):
    if len(side_out_ref.shape) == 2:
        side_out_ref[...] = side_ref[...].astype(BF16)
    else:
        n_tiles, _, col_tile = side_out_ref.shape
        for j in range(n_tiles):
            side_out_ref[j] = side_ref[:, j * col_tile:(j + 1) * col_tile].astype(BF16)


def _mixer_in_kernel(h_ref, gpre_ref, gv_ref, w_hbm, side_ref, u_ref, vn_ref, p_ref, glu_ref, side_out_ref,
                     w_ref, stage_ref, sem, *, layer):
    _side_cast(side_ref, side_out_ref)

    @pl.when(pl.program_id(0) == 0)
    def _():
        _load_cast_weight(w_hbm, layer, w_ref, stage_ref, sem)

    h = h_ref[...]
    if h.dtype != BF16:
        h = _rms(h, gpre_ref[...]).astype(BF16)
    u_ref[...] = jax.nn.gelu(_dot(h, w_ref[:, 0:GMLP_WIDTH])).astype(u_ref.dtype)
    v = jax.nn.gelu(_dot(h, w_ref[:, GMLP_WIDTH:2 * GMLP_WIDTH]))
    for hd in range(GMLP_HEADS):
        sl = slice(hd * HEAD_DIM, (hd + 1) * HEAD_DIM)
        vn_ref[:, sl] = _layer_norm(v[:, sl], gv_ref[:, sl]).astype(vn_ref.dtype)
    c0 = 2 * GMLP_WIDTH
    p_ref[...] = _dot(h, w_ref[:, c0:c0 + POOL_WIDTH])
    c1 = c0 + POOL_WIDTH
    c_val = _dot(h, w_ref[:, c1:c1 + CONV_WIDTH])
    c_gate = _dot(h, w_ref[:, c1 + CONV_WIDTH:c1 + 2 * CONV_WIDTH])
    glu_ref[...] = c_val * jax.nn.sigmoid(c_gate)


def _mixer_in(h, g_pre, w_in, g_v, side, layer, tm=512):
    t, d = h.shape
    n = t // tm
    row = lambda c: pl.BlockSpec((tm, c), lambda i: (i, 0))
    side_in, side_out, side_shape = _side_cast_specs(side, layer, n)
    return pl.pallas_call(
        functools.partial(_mixer_in_kernel, layer=layer),
        grid=(n,),
        in_specs=[row(d), _layer_spec(layer, (1, d)), _layer_spec(layer, (1, GMLP_WIDTH)), HBM_SPEC, side_in],
        out_specs=[row(GMLP_WIDTH), row(GMLP_WIDTH), row(POOL_WIDTH), row(CONV_WIDTH), side_out],
        out_shape=[
            jax.ShapeDtypeStruct((t, GMLP_WIDTH), BF16),
            jax.ShapeDtypeStruct((t, GMLP_WIDTH), BF16),
            jax.ShapeDtypeStruct((t, POOL_WIDTH), F32),
            jax.ShapeDtypeStruct((t, CONV_WIDTH), F32),
            side_shape,
        ],
        scratch_shapes=_weight_scratch(d, IN_COLS, chunk_rows=128),
        compiler_params=_params("arbitrary"),
        name="mixer_in",
    )(h, g_pre, g_v, w_in, side)


def _mixer_chunk(c, tile, u_ref, vn_ref, p_ref, wmask_ref, bst_ref, wp_ref, sp_ref, bdw_ref, lng_ref, lnb_ref,
                 y_ref, pext_ref, cext_ref, shift_ref, wtap_ref, conv_ref, *, tm, tiles_per_seq, conv_rows):
    row0 = c * CHUNK
    rows = slice(row0, row0 + CHUNK)

    for hd in range(GMLP_HEADS):
        cols = slice(hd * HEAD_DIM, (hd + 1) * HEAD_DIM)
        mixed = _dot(wmask_ref[hd], vn_ref[rows, cols]) + bst_ref[:, hd:hd + 1]
        y_ref[rows, cols] = (u_ref[rows, cols].astype(F32) * mixed).astype(y_ref.dtype)

    pos = (lax.broadcasted_iota(jnp.int32, (CHUNK, 1), 0) + (tile % tiles_per_seq) * tm + row0).astype(F32)
    for g, w in enumerate(POOL_WINDOWS):
        cols = slice(g * POOL_GROUP_WIDTH, (g + 1) * POOL_GROUP_WIDTH)
        win = pext_ref[HALO + row0 - (w - 1):HALO + row0 + CHUNK, cols]
        span = 1
        while span < w:
            win = win[span:, :] + win[:-span, :]
            span *= 2
        pooled = win / jnp.minimum(pos + 1.0, float(w)) - p_ref[rows, cols]
        out = _dot(pooled.astype(BF16), wp_ref[g].astype(BF16)) * sp_ref[:, cols]
        y_ref[rows, GMLP_WIDTH + g * POOL_GROUP_WIDTH:GMLP_WIDTH + (g + 1) * POOL_GROUP_WIDTH] = (
            out.astype(y_ref.dtype))

    n_shift_rows = CHUNK + HALO - SUBLANES
    for s in range(1, SUBLANES):
        shift_ref[s - 1, 0:n_shift_rows, :] = cext_ref[row0 + s:row0 + s + n_shift_rows, :]
    base = HALO - (CONV_K - 1)
    bias = jnp.broadcast_to(bdw_ref[...], (SUBLANES, CONV_WIDTH))
    for r in range(CHUNK // conv_rows):
        acc = [bias] * (conv_rows // SUBLANES)
        for j in range(CONV_K):
            s, start = (base + j) % SUBLANES, r * conv_rows + (base + j) // SUBLANES * SUBLANES
            w_tap = wtap_ref[j]
            for q in range(conv_rows // SUBLANES):
                lo = start + q * SUBLANES
                if s == 0:
                    tap = cext_ref[row0 + lo:row0 + lo + SUBLANES, :]
                else:
                    tap = shift_ref[s - 1, lo:lo + SUBLANES, :]
                acc[q] = acc[q] + w_tap * tap
        for q in range(conv_rows // SUBLANES):
            lo = r * conv_rows + q * SUBLANES
            conv_ref[lo:lo + SUBLANES, :] = acc[q]
    yc = jax.nn.silu(_layer_norm(conv_ref[...], lng_ref[...], lnb_ref[...]))
    y_ref[rows, GMLP_WIDTH + POOL_WIDTH:] = yc.astype(y_ref.dtype)


def _mixer_core_kernel(u_ref, vn_ref, p_ref, ph_ref, c_ref, ch_ref, ws_ref, bst_ref, wp_ref, sp_ref, wdw_ref,
                       bdw_ref, lng_ref, lnb_ref, y_ref, pext_ref, cext_ref, shift_ref, wtap_ref, conv_ref,
                       wmask_ref, *, tm, tiles_per_seq):
    tile = pl.program_id(0)
    for j in range(CONV_K):
        wtap_ref[j] = jnp.broadcast_to(wdw_ref[j:j + 1, :], (SUBLANES, CONV_WIDTH))
    tri = (lax.broadcasted_iota(jnp.int32, (CHUNK, CHUNK), 0)
           >= lax.broadcasted_iota(jnp.int32, (CHUNK, CHUNK), 1))
    for hd in range(GMLP_HEADS):
        wmask_ref[hd] = jnp.where(tri, ws_ref[hd], 0.0).astype(BF16)

    keep = jnp.where(tile % tiles_per_seq == 0, 0.0, 1.0).astype(F32)
    pext_ref[0:HALO, :] = ph_ref[...] * keep
    pext_ref[HALO:HALO + tm, :] = p_ref[...]
    cext_ref[0:HALO, :] = ch_ref[...] * keep
    cext_ref[HALO:HALO + tm, :] = c_ref[...]

    for c in range(tm // CHUNK):
        _mixer_chunk(c, tile, u_ref, vn_ref, p_ref, wmask_ref, bst_ref, wp_ref, sp_ref, bdw_ref, lng_ref,
                     lnb_ref, y_ref, pext_ref, cext_ref, shift_ref, wtap_ref, conv_ref,
                     tm=tm, tiles_per_seq=tiles_per_seq, conv_rows=32)


def _mixer_core(u, vn, p, glu, w_s, b_s_t, w_pool, s_pool, w_dw, b_dw, ln_g, ln_b, layer, seq, tm=512):
    t = u.shape[0]
    row = lambda n: pl.BlockSpec((tm, n), lambda i: (i, 0))
    halo = lambda n: pl.BlockSpec((HALO, n), lambda i: (jnp.maximum(i * (tm // HALO) - 1, 0), 0))
    kernel = functools.partial(_mixer_core_kernel, tm=tm, tiles_per_seq=seq // tm)
    return pl.pallas_call(
        kernel,
        grid=(t // tm,),
        in_specs=[
            row(GMLP_WIDTH), row(GMLP_WIDTH),
            row(POOL_WIDTH), halo(POOL_WIDTH),
            row(CONV_WIDTH), halo(CONV_WIDTH),
            _layer_spec(layer, (GMLP_HEADS, CHUNK, CHUNK)),
            _layer_spec(layer, (CHUNK, GMLP_HEADS)),
            _layer_spec(layer, (len(POOL_WINDOWS), POOL_GROUP_WIDTH, POOL_GROUP_WIDTH)),
            _layer_spec(layer, (1, POOL_WIDTH)),
            _layer_spec(layer, (CONV_K, CONV_WIDTH)),
            _layer_spec(layer, (1, CONV_WIDTH)),
            _layer_spec(layer, (1, CONV_WIDTH)),
            _layer_spec(layer, (1, CONV_WIDTH)),
        ],
        out_specs=row(D_MODEL),
        out_shape=jax.ShapeDtypeStruct((t, D_MODEL), BF16),
        scratch_shapes=[
            pltpu.VMEM((HALO + tm, POOL_WIDTH), F32),
            pltpu.VMEM((HALO + tm, CONV_WIDTH), F32),
            pltpu.VMEM((SUBLANES - 1, HALO + CHUNK, CONV_WIDTH), F32),
            pltpu.VMEM((CONV_K, SUBLANES, CONV_WIDTH), F32),
            pltpu.VMEM((CHUNK, CONV_WIDTH), F32),
            pltpu.VMEM((GMLP_HEADS, CHUNK, CHUNK), BF16),
        ],
        compiler_params=_params("parallel"),
        name="mixer_core",
    )(u, vn, p, p, glu, glu, w_s, b_s_t, w_pool, s_pool, w_dw, b_dw, ln_g, ln_b)


def _proj_res_kernel(a_ref, x_ref, gpost_ref, gnext_ref, w_hbm, xo_ref, ho_ref, w_ref, stage_ref, sem, *,
                     layer):
    def finish(rows, h):
        xn = x_ref[rows, :] + _rms(h, gpost_ref[...])
        xo_ref[rows, :] = xn
        ho_ref[rows, :] = _rms(xn, gnext_ref[...]).astype(ho_ref.dtype)

    @pl.when(pl.program_id(0) == 0)
    def _():
        finish(slice(None), _load_cast_matmul(a_ref, w_hbm, layer, w_ref, stage_ref, sem))

    @pl.when(pl.program_id(0) > 0)
    def _():
        chunk = a_ref.shape[0] // PROJ_ROW_CHUNKS
        for c in range(PROJ_ROW_CHUNKS):
            rows = slice(c * chunk, (c + 1) * chunk)
            finish(rows, _dot(a_ref[rows, :], w_ref[...]))


def _proj_res(a, w, x, g_post, g_next, layer, tm=512):
    t, k = a.shape
    d = x.shape[1]
    row = lambda c: pl.BlockSpec((tm, c), lambda i: (i, 0))
    return pl.pallas_call(
        functools.partial(_proj_res_kernel, layer=layer),
        grid=(t // tm,),
        in_specs=[row(k), row(d), _layer_spec(layer, (1, d)), _layer_spec(layer, (1, d)), HBM_SPEC],
        out_specs=[row(d), row(d)],
        out_shape=[jax.ShapeDtypeStruct((t, d), F32), jax.ShapeDtypeStruct((t, d), BF16)],
        scratch_shapes=_weight_scratch(k, d),
        compiler_params=_params("arbitrary"),
        name="proj_res",
    )(a, x, g_post, g_next, w)


def _kv_kernel(m_ref, g_ref, wk_ref, wv_ref, k_ref, v_ref, mn_ref):
    @pl.when(pl.program_id(0) == 0)
    def _():
        mn_ref[...] = _rms(m_ref[...], g_ref[...]).astype(BF16)

    k_ref[...] = _dot(mn_ref[...], wk_ref[...].astype(BF16)).astype(k_ref.dtype)
    v_ref[...] = _dot(mn_ref[...], wv_ref[...].astype(BF16)).astype(v_ref.dtype)


def _kv_proj(mem, g_mem, w_k, w_v, layer, tn=256):
    t, d = mem.shape
    full = pl.BlockSpec((t, d), lambda j: (0, 0))
    wcol = pl.BlockSpec((None, d, tn), lambda j: (layer, 0, j))
    ocol = pl.BlockSpec((t, tn), lambda j: (0, j))
    return pl.pallas_call(
        _kv_kernel,
        grid=(d // tn,),
        in_specs=[full, _layer_spec(layer, (1, d)), wcol, wcol],
        out_specs=[ocol, ocol],
        out_shape=[jax.ShapeDtypeStruct((t, d), BF16)] * 2,
        scratch_shapes=[pltpu.VMEM((t, d), BF16)],
        compiler_params=_params("arbitrary"),
        name="kv_proj",
    )(mem, g_mem, w_k, w_v)


def _attn_kernel(h_ref, k_ref, v_ref, w_hbm, side_ref, o_ref, side_out_ref, wq_ref, stage_ref, sem, *, layer):
    _side_cast(side_ref, side_out_ref)

    def attend(q):
        q = q.astype(BF16)
        scale = XATTN_HEAD_DIM ** -0.5
        for hd in range(XATTN_HEADS):
            cols = slice(hd * XATTN_HEAD_DIM, (hd + 1) * XATTN_HEAD_DIM)
            s = lax.dot_general(q[:, cols], k_ref[:, cols], (((1,), (1,)), ((), ())),
                                preferred_element_type=F32) * scale
            e = jnp.exp(s - jnp.max(s, axis=-1, keepdims=True))
            o = _dot(e.astype(BF16), v_ref[:, cols]) / jnp.sum(e, axis=-1, keepdims=True)
            o_ref[:, cols] = o.astype(o_ref.dtype)

    @pl.when(pl.program_id(0) == 0)
    def _():
        attend(_load_cast_matmul(h_ref, w_hbm, layer, wq_ref, stage_ref, sem))

    @pl.when(pl.program_id(0) > 0)
    def _():
        attend(_dot(h_ref[...], wq_ref[...]))


def _attention(h, w_q, k, v, side, layer, seq, mem_len, tm=512):
    t, d = h.shape
    n = t // tm
    tiles_per_seq = seq // tm
    row = pl.BlockSpec((tm, d), lambda i: (i, 0))
    mem_rows = pl.BlockSpec((mem_len, d), lambda i: (i // tiles_per_seq, 0))
    side_in, side_out, side_shape = _side_cast_specs(side, layer, n, col_tile=FFN_CHUNK)
    return pl.pallas_call(
        functools.partial(_attn_kernel, layer=layer),
        grid=(n,),
        in_specs=[row, mem_rows, mem_rows, HBM_SPEC, side_in],
        out_specs=[row, side_out],
        out_shape=[jax.ShapeDtypeStruct((t, d), BF16), side_shape],
        scratch_shapes=_weight_scratch(d, d),
        compiler_params=_params("arbitrary"),
        name="attention",
    )(h, k, v, w_q, side)


def _ffn_kernel(h_ref, gpost_ref, gnext_ref, x_hbm, wu_hbm, wd_hbm, xo_ref, *rest, n_tiles):
    *ho_ref, wu_buf, wd_buf, sem, x_sem, acc_ref = rest
    i = pl.program_id(0)
    n_chunks, _, tf = wu_hbm.shape
    tm = xo_ref.shape[0]

    def copies(f, slot):
        row0 = pl.multiple_of(f * tf, tf)
        return (pltpu.make_async_copy(wu_hbm.at[f], wu_buf.at[slot], sem.at[0, slot]),
                pltpu.make_async_copy(wd_hbm.at[pl.ds(row0, tf), :], wd_buf.at[slot], sem.at[1, slot]))

    def start(f, slot):
        for cp in copies(f, slot):
            cp.start()

    def wait(f, slot):
        for cp in copies(f, slot):
            cp.wait()

    @pl.when(i == 0)
    def _():
        start(0, 0)

    x_copy = pltpu.make_async_copy(x_hbm.at[pl.ds(pl.multiple_of(i * tm, tm), tm), :], xo_ref, x_sem.at[0])
    x_copy.start()

    def hidden(rows, slot):
        a = jnp.square(jnp.maximum(_dot(h_ref[rows, :], wu_buf[slot]), 0.0))
        return _dot(a.astype(BF16), wd_buf[slot])

    def finish(rows, acc):
        xn = xo_ref[rows, :] + _rms(acc, gpost_ref[...])
        xo_ref[rows, :] = xn
        if ho_ref:
            ho_ref[0][rows, :] = _rms(xn, gnext_ref[...]).astype(ho_ref[0].dtype)

    def pair(p, carry, first=False, last=False):
        for slot in (0, 1):
            f = 2 * p + slot
            start((f + 1) % n_chunks, 1 - slot)
            wait(f, slot)
            if last and slot == 1:
                x_copy.wait()
                block = tm // FFN_FINAL_ROW_BLOCKS
                for r in range(FFN_FINAL_ROW_BLOCKS):
                    rows = slice(r * block, (r + 1) * block)
                    finish(rows, acc_ref[rows, :] + hidden(rows, slot))
            elif first and slot == 0:
                acc_ref[...] = hidden(slice(None), slot)
            else:
                acc_ref[...] += hidden(slice(None), slot)
        return carry

    n_pairs = n_chunks // 2
    pair(0, 0, first=True)
    lax.fori_loop(1, n_pairs - 1, pair, 0)
    pair(n_pairs - 1, 0, last=True)

    @pl.when(i == n_tiles - 1)
    def _():
        wait(0, 0)


def _ffn(h, w_up, w_down, x, g_post, g_next, layer, next_layer, tm=1024):
    t, d = h.shape
    n = t // tm
    tf = w_up.shape[-1]
    row = pl.BlockSpec((tm, d), lambda i: (i, 0))
    emit_next = next_layer is not None
    out_specs = [row] + ([row] if emit_next else [])
    out_shape = [jax.ShapeDtypeStruct((t, d), F32)] + ([jax.ShapeDtypeStruct((t, d), BF16)] if emit_next else [])
    outs = pl.pallas_call(
        functools.partial(_ffn_kernel, n_tiles=n),
        grid=(n,),
        in_specs=[row, _layer_spec(layer, (1, d)), _layer_spec(next_layer if emit_next else layer, (1, d)),
                  HBM_SPEC, HBM_SPEC, HBM_SPEC],
        out_specs=out_specs,
        out_shape=out_shape,
        scratch_shapes=[pltpu.VMEM((2, d, tf), BF16), pltpu.VMEM((2, tf, d), BF16),
                        pltpu.SemaphoreType.DMA((2, 2)), pltpu.SemaphoreType.DMA((1,)),
                        pltpu.VMEM((tm, d), F32)],
        compiler_params=_params("arbitrary"),
        name="ffn",
    )(h, g_post, g_next, x, w_up, w_down)
    return (outs[0], outs[1]) if emit_next else (outs[0], None)


def kernel(x, mem, norm_mix_pre, norm_mix_post, w_in, w_out, gmlp_v_gain, w_spatial, b_spatial, w_pool,
           s_pool, w_dw, b_dw, conv_ln_g, conv_ln_b, norm_xattn_pre, norm_mem, norm_xattn_post, w_q, w_k,
           w_v, w_o, norm_ffn_pre, norm_ffn_post, w_up, w_down):
    batch, seq, d = x.shape
    mem_len = mem.shape[1]
    depth = w_in.shape[0]
    t = batch * seq

    vec = lambda a: a.reshape(depth, 1, -1)
    norm_mix_pre, norm_mix_post = vec(norm_mix_pre), vec(norm_mix_post)
    norm_xattn_pre, norm_mem, norm_xattn_post = vec(norm_xattn_pre), vec(norm_mem), vec(norm_xattn_post)
    norm_ffn_pre, norm_ffn_post = vec(norm_ffn_pre), vec(norm_ffn_post)
    g_v, s_pool, b_dw = vec(gmlp_v_gain), vec(s_pool), vec(b_dw)
    conv_ln_g, conv_ln_b = vec(conv_ln_g), vec(conv_ln_b)
    b_s_t = jnp.swapaxes(b_spatial, 1, 2)

    xf = x.reshape(t, d)
    memf = mem.reshape(batch * mem_len, d)

    h = xf
    for l in range(depth):
        u, vn, p, glu, w_down_bf = _mixer_in(h, norm_mix_pre, w_in, g_v, w_down, l)
        y = _mixer_core(u, vn, p, glu, w_spatial, b_s_t, w_pool, s_pool, w_dw, b_dw, conv_ln_g, conv_ln_b,
                        l, seq)
        xf, h = _proj_res(y, w_out, xf, norm_mix_post, norm_xattn_pre, l)
        k, v = _kv_proj(memf, norm_mem, w_k, w_v, l)
        a, w_up_bf = _attention(h, w_q, k, v, w_up, l, seq, mem_len)
        xf, h = _proj_res(a, w_o, xf, norm_xattn_post, norm_ffn_pre, l)
        nxt = l + 1 if l + 1 < depth else None
        xf, h = _ffn(h, w_up_bf, w_down_bf, xf, norm_ffn_post, norm_mix_pre, l, nxt)
    return xf.reshape(batch, seq, d)
```

```python
import functools

import jax
import jax.numpy as jnp
from jax import lax
from jax.experimental import pallas as pl
from jax.experimental.pallas import tpu as pltpu

BF16 = jnp.bfloat16
F32 = jnp.float32

D_MODEL = 2048
HEAD_DIM = 128
CHUNK = 128
GMLP_WIDTH = 1024
GMLP_HEADS = GMLP_WIDTH // HEAD_DIM
POOL_WIDTH = 512
POOL_WINDOWS = (2, 4, 8, 16)
POOL_GROUP_WIDTH = POOL_WIDTH // len(POOL_WINDOWS)
CONV_WIDTH = 512
CONV_K = 31
IN_COLS = 2 * GMLP_WIDTH + POOL_WIDTH + 2 * CONV_WIDTH
XATTN_HEADS = 4
XATTN_HEAD_DIM = D_MODEL // XATTN_HEADS
RMS_EPS = 1e-6
LN_EPS = 1e-5

HALO = 32
SUBLANES = 8
VMEM_LIMIT_BYTES = 56 * 1024 * 1024
PROJ_ROW_CHUNKS = 4
FFN_CHUNK = 512
FFN_FINAL_ROW_BLOCKS = 4
WEIGHT_CHUNK_ROWS = 256


def _params(*semantics):
    return pltpu.CompilerParams(dimension_semantics=semantics, vmem_limit_bytes=VMEM_LIMIT_BYTES)


def _rms(x, g):
    return x * lax.rsqrt(jnp.mean(x * x, axis=-1, keepdims=True) + RMS_EPS) * g


def _layer_norm(x, g, b=None):
    mu = jnp.mean(x, axis=-1, keepdims=True)
    xc = x - mu
    var = jnp.mean(xc * xc, axis=-1, keepdims=True)
    y = xc * lax.rsqrt(var + LN_EPS) * g
    if b is not None:
        y = y + b
    return y


def _dot(a, b):
    return jnp.dot(a, b, preferred_element_type=F32)


def _layer_spec(layer, shape):
    zeros = (0,) * len(shape)
    return pl.BlockSpec((None,) + tuple(shape), lambda *_: (layer,) + zeros)


def _load_cast_weight(w_hbm, layer, wbf_ref, stage_ref, sem):
    chunk_rows = stage_ref.shape[1]
    n_chunks = wbf_ref.shape[0] // chunk_rows

    def copy(c):
        return pltpu.make_async_copy(w_hbm.at[layer, pl.ds(c * chunk_rows, chunk_rows), :],
                                     stage_ref.at[c % 2], sem.at[c % 2])

    copy(0).start()
    for c in range(n_chunks):
        if c + 1 < n_chunks:
            copy(c + 1).start()
        copy(c).wait()
        wbf_ref[c * chunk_rows:(c + 1) * chunk_rows, :] = stage_ref[c % 2].astype(BF16)


def _load_cast_matmul(a_ref, w_hbm, layer, wbf_ref, stage_ref, sem):
    chunk_rows = stage_ref.shape[1]
    n_chunks = wbf_ref.shape[0] // chunk_rows

    def copy(c):
        return pltpu.make_async_copy(w_hbm.at[layer, pl.ds(c * chunk_rows, chunk_rows), :],
                                     stage_ref.at[c % 2], sem.at[c % 2])

    copy(0).start()
    acc = None
    for c in range(n_chunks):
        if c + 1 < n_chunks:
            copy(c + 1).start()
        copy(c).wait()
        rows = slice(c * chunk_rows, (c + 1) * chunk_rows)
        wbf_ref[rows, :] = stage_ref[c % 2].astype(BF16)
        part = _dot(a_ref[:, rows], wbf_ref[rows, :])
        acc = part if acc is None else acc + part
    return acc


def _weight_scratch(k, n, chunk_rows=WEIGHT_CHUNK_ROWS):
    return [pltpu.VMEM((k, n), BF16), pltpu.VMEM((2, chunk_rows, n), F32), pltpu.SemaphoreType.DMA((2,))]


HBM_SPEC = pl.BlockSpec(memory_space=pl.ANY)


def _side_cast_specs(w, layer, n_steps, col_tile=None):
    _, rows, cols = w.shape
    rs = rows // n_steps
    in_spec = pl.BlockSpec((None, rs, cols), lambda i: (layer, i, 0))
    if col_tile is None:
        return in_spec, pl.BlockSpec((rs, cols), lambda i: (i, 0)), jax.ShapeDtypeStruct((rows, cols), BF16)
    n_tiles = cols // col_tile
    out_spec = pl.BlockSpec((n_tiles, rs, col_tile), lambda i: (0, i, 0))
    return in_spec, out_spec, jax.ShapeDtypeStruct((n_tiles, rows, col_tile), BF16)


def _side_cast(side_ref, side_out_ref):
    if len(side_out_ref.shape) == 2:
        side_out_ref[...] = side_ref[...].astype(BF16)
    else:
        n_tiles, _, col_tile = side_out_ref.shape
        for j in range(n_tiles):
            side_out_ref[j] = side_ref[:, j * col_tile:(j + 1) * col_tile].astype(BF16)


def _mixer_in_kernel(x_ref, gpre_ref, gv_ref, w_hbm, side_ref, u_ref, vn_ref, p_ref, glu_ref, side_out_ref,
                     w_ref, stage_ref, sem, *, layer):
    _side_cast(side_ref, side_out_ref)

    @pl.when(pl.program_id(0) == 0)
    def _():
        _load_cast_weight(w_hbm, layer, w_ref, stage_ref, sem)

    h = _rms(x_ref[...], gpre_ref[...]).astype(BF16)
    u_ref[...] = jax.nn.gelu(_dot(h, w_ref[:, 0:GMLP_WIDTH])).astype(u_ref.dtype)
    v = jax.nn.gelu(_dot(h, w_ref[:, GMLP_WIDTH:2 * GMLP_WIDTH]))
    for hd in range(GMLP_HEADS):
        sl = slice(hd * HEAD_DIM, (hd + 1) * HEAD_DIM)
        vn_ref[:, sl] = _layer_norm(v[:, sl], gv_ref[:, sl]).astype(vn_ref.dtype)
    c0 = 2 * GMLP_WIDTH
    p_ref[...] = _dot(h, w_ref[:, c0:c0 + POOL_WIDTH])
    c1 = c0 + POOL_WIDTH
    c_val = _dot(h, w_ref[:, c1:c1 + CONV_WIDTH])
    c_gate = _dot(h, w_ref[:, c1 + CONV_WIDTH:c1 + 2 * CONV_WIDTH])
    glu_ref[...] = c_val * jax.nn.sigmoid(c_gate)


def _mixer_in(x, g_pre, w_in, g_v, side, layer, tm=512):
    t, d = x.shape
    n = t // tm
    row = lambda c: pl.BlockSpec((tm, c), lambda i: (i, 0))
    side_in, side_out, side_shape = _side_cast_specs(side, layer, n)
    return pl.pallas_call(
        functools.partial(_mixer_in_kernel, layer=layer),
        grid=(n,),
        in_specs=[row(d), _layer_spec(layer, (1, d)), _layer_spec(layer, (1, GMLP_WIDTH)), HBM_SPEC, side_in],
        out_specs=[row(GMLP_WIDTH), row(GMLP_WIDTH), row(POOL_WIDTH), row(CONV_WIDTH), side_out],
        out_shape=[
            jax.ShapeDtypeStruct((t, GMLP_WIDTH), BF16),
            jax.ShapeDtypeStruct((t, GMLP_WIDTH), BF16),
            jax.ShapeDtypeStruct((t, POOL_WIDTH), F32),
            jax.ShapeDtypeStruct((t, CONV_WIDTH), F32),
            side_shape,
        ],
        scratch_shapes=_weight_scratch(d, IN_COLS, chunk_rows=128),
        compiler_params=_params("arbitrary"),
        name="mixer_in",
    )(x, g_pre, g_v, w_in, side)


def _mixer_chunk(c, tile, u_ref, vn_ref, p_ref, ws_ref, bst_ref, wp_ref, sp_ref, bdw_ref, lng_ref, lnb_ref,
                 y_ref, pext_ref, cext_ref, shift_ref, wtap_ref, conv_ref, *, tm, tiles_per_seq, conv_rows):
    row0 = c * CHUNK
    rows = slice(row0, row0 + CHUNK)

    tri = (lax.broadcasted_iota(jnp.int32, (CHUNK, CHUNK), 0)
           >= lax.broadcasted_iota(jnp.int32, (CHUNK, CHUNK), 1))
    for hd in range(GMLP_HEADS):
        w_mask = jnp.where(tri, ws_ref[hd], 0.0).astype(BF16)
        cols = slice(hd * HEAD_DIM, (hd + 1) * HEAD_DIM)
        mixed = _dot(w_mask, vn_ref[rows, cols]) + bst_ref[:, hd:hd + 1]
        y_ref[rows, cols] = (u_ref[rows, cols].astype(F32) * mixed).astype(y_ref.dtype)

    pos = (lax.broadcasted_iota(jnp.int32, (CHUNK, 1), 0) + (tile % tiles_per_seq) * tm + row0).astype(F32)
    for g, w in enumerate(POOL_WINDOWS):
        cols = slice(g * POOL_GROUP_WIDTH, (g + 1) * POOL_GROUP_WIDTH)
        win = pext_ref[HALO + row0 - (w - 1):HALO + row0 + CHUNK, cols]
        span = 1
        while span < w:
            win = win[span:, :] + win[:-span, :]
            span *= 2
        pooled = win / jnp.minimum(pos + 1.0, float(w)) - p_ref[rows, cols]
        out = _dot(pooled.astype(BF16), wp_ref[g].astype(BF16)) * sp_ref[:, cols]
        y_ref[rows, GMLP_WIDTH + g * POOL_GROUP_WIDTH:GMLP_WIDTH + (g + 1) * POOL_GROUP_WIDTH] = (
            out.astype(y_ref.dtype))

    n_shift_rows = CHUNK + HALO - SUBLANES
    for s in range(1, SUBLANES):
        shift_ref[s - 1, 0:n_shift_rows, :] = cext_ref[row0 + s:row0 + s + n_shift_rows, :]
    base = HALO - (CONV_K - 1)
    bias = jnp.broadcast_to(bdw_ref[...], (SUBLANES, CONV_WIDTH))
    for r in range(CHUNK // conv_rows):
        acc = [bias] * (conv_rows // SUBLANES)
        for j in range(CONV_K):
            s, start = (base + j) % SUBLANES, r * conv_rows + (base + j) // SUBLANES * SUBLANES
            w_tap = wtap_ref[j]
            for q in range(conv_rows // SUBLANES):
                lo = start + q * SUBLANES
                if s == 0:
                    tap = cext_ref[row0 + lo:row0 + lo + SUBLANES, :]
                else:
                    tap = shift_ref[s - 1, lo:lo + SUBLANES, :]
                acc[q] = acc[q] + w_tap * tap
        for q in range(conv_rows // SUBLANES):
            lo = r * conv_rows + q * SUBLANES
            conv_ref[lo:lo + SUBLANES, :] = acc[q]
    yc = jax.nn.silu(_layer_norm(conv_ref[...], lng_ref[...], lnb_ref[...]))
    y_ref[rows, GMLP_WIDTH + POOL_WIDTH:] = yc.astype(y_ref.dtype)


def _mixer_core_kernel(u_ref, vn_ref, p_ref, ph_ref, c_ref, ch_ref, ws_ref, bst_ref, wp_ref, sp_ref, wdw_ref,
                       bdw_ref, lng_ref, lnb_ref, y_ref, pext_ref, cext_ref, shift_ref, wtap_ref, conv_ref,
                       *, tm, tiles_per_seq):
    tile = pl.program_id(0)
    for j in range(CONV_K):
        wtap_ref[j] = jnp.broadcast_to(wdw_ref[j:j + 1, :], (SUBLANES, CONV_WIDTH))

    keep = jnp.where(tile % tiles_per_seq == 0, 0.0, 1.0).astype(F32)
    pext_ref[0:HALO, :] = ph_ref[...] * keep
    pext_ref[HALO:HALO + tm, :] = p_ref[...]
    cext_ref[0:HALO, :] = ch_ref[...] * keep
    cext_ref[HALO:HALO + tm, :] = c_ref[...]

    for c in range(tm // CHUNK):
        _mixer_chunk(c, tile, u_ref, vn_ref, p_ref, ws_ref, bst_ref, wp_ref, sp_ref, bdw_ref, lng_ref, lnb_ref,
                     y_ref, pext_ref, cext_ref, shift_ref, wtap_ref, conv_ref,
                     tm=tm, tiles_per_seq=tiles_per_seq, conv_rows=32)


def _mixer_core(u, vn, p, glu, w_s, b_s_t, w_pool, s_pool, w_dw, b_dw, ln_g, ln_b, layer, seq, tm=512):
    t = u.shape[0]
    row = lambda n: pl.BlockSpec((tm, n), lambda i: (i, 0))
    halo = lambda n: pl.BlockSpec((HALO, n), lambda i: (jnp.maximum(i * (tm // HALO) - 1, 0), 0))
    kernel = functools.partial(_mixer_core_kernel, tm=tm, tiles_per_seq=seq // tm)
    return pl.pallas_call(
        kernel,
        grid=(t // tm,),
        in_specs=[
            row(GMLP_WIDTH), row(GMLP_WIDTH),
            row(POOL_WIDTH), halo(POOL_WIDTH),
            row(CONV_WIDTH), halo(CONV_WIDTH),
            _layer_spec(layer, (GMLP_HEADS, CHUNK, CHUNK)),
            _layer_spec(layer, (CHUNK, GMLP_HEADS)),
            _layer_spec(layer, (len(POOL_WINDOWS), POOL_GROUP_WIDTH, POOL_GROUP_WIDTH)),
            _layer_spec(layer, (1, POOL_WIDTH)),
            _layer_spec(layer, (CONV_K, CONV_WIDTH)),
            _layer_spec(layer, (1, CONV_WIDTH)),
            _layer_spec(layer, (1, CONV_WIDTH)),
            _layer_spec(layer, (1, CONV_WIDTH)),
        ],
        out_specs=row(D_MODEL),
        out_shape=jax.ShapeDtypeStruct((t, D_MODEL), BF16),
        scratch_shapes=[
            pltpu.VMEM((HALO + tm, POOL_WIDTH), F32),
            pltpu.VMEM((HALO + tm, CONV_WIDTH), F32),
            pltpu.VMEM((SUBLANES - 1, HALO + CHUNK, CONV_WIDTH), F32),
            pltpu.VMEM((CONV_K, SUBLANES, CONV_WIDTH), F32),
            pltpu.VMEM((CHUNK, CONV_WIDTH), F32),
        ],
        compiler_params=_params("parallel"),
        name="mixer_core",
    )(u, vn, p, p, glu, glu, w_s, b_s_t, w_pool, s_pool, w_dw, b_dw, ln_g, ln_b)


def _proj_res_kernel(a_ref, x_ref, gpost_ref, gnext_ref, w_hbm, xo_ref, ho_ref, w_ref, stage_ref, sem, *,
                     layer):
    def finish(rows, h):
        xn = x_ref[rows, :] + _rms(h, gpost_ref[...])
        xo_ref[rows, :] = xn
        ho_ref[rows, :] = _rms(xn, gnext_ref[...]).astype(ho_ref.dtype)

    @pl.when(pl.program_id(0) == 0)
    def _():
        finish(slice(None), _load_cast_matmul(a_ref, w_hbm, layer, w_ref, stage_ref, sem))

    @pl.when(pl.program_id(0) > 0)
    def _():
        chunk = a_ref.shape[0] // PROJ_ROW_CHUNKS
        for c in range(PROJ_ROW_CHUNKS):
            rows = slice(c * chunk, (c + 1) * chunk)
            finish(rows, _dot(a_ref[rows, :], w_ref[...]))


def _proj_res(a, w, x, g_post, g_next, layer, tm=512):
    t, k = a.shape
    d = x.shape[1]
    row = lambda c: pl.BlockSpec((tm, c), lambda i: (i, 0))
    return pl.pallas_call(
        functools.partial(_proj_res_kernel, layer=layer),
        grid=(t // tm,),
        in_specs=[row(k), row(d), _layer_spec(layer, (1, d)), _layer_spec(layer, (1, d)), HBM_SPEC],
        out_specs=[row(d), row(d)],
        out_shape=[jax.ShapeDtypeStruct((t, d), F32), jax.ShapeDtypeStruct((t, d), BF16)],
        scratch_shapes=_weight_scratch(k, d),
        compiler_params=_params("arbitrary"),
        name="proj_res",
    )(a, x, g_post, g_next, w)


def _kv_kernel(m_ref, g_ref, wk_ref, wv_ref, k_ref, v_ref, mn_ref):
    @pl.when(pl.program_id(0) == 0)
    def _():
        mn_ref[...] = _rms(m_ref[...], g_ref[...]).astype(BF16)

    k_ref[...] = _dot(mn_ref[...], wk_ref[...].astype(BF16)).astype(k_ref.dtype)
    v_ref[...] = _dot(mn_ref[...], wv_ref[...].astype(BF16)).astype(v_ref.dtype)


def _kv_proj(mem, g_mem, w_k, w_v, layer, tn=256):
    t, d = mem.shape
    full = pl.BlockSpec((t, d), lambda j: (0, 0))
    wcol = pl.BlockSpec((None, d, tn), lambda j: (layer, 0, j))
    ocol = pl.BlockSpec((t, tn), lambda j: (0, j))
    return pl.pallas_call(
        _kv_kernel,
        grid=(d // tn,),
        in_specs=[full, _layer_spec(layer, (1, d)), wcol, wcol],
        out_specs=[ocol, ocol],
        out_shape=[jax.ShapeDtypeStruct((t, d), BF16)] * 2,
        scratch_shapes=[pltpu.VMEM((t, d), BF16)],
        compiler_params=_params("arbitrary"),
        name="kv_proj",
    )(mem, g_mem, w_k, w_v)


def _attn_kernel(h_ref, k_ref, v_ref, w_hbm, side_ref, o_ref, side_out_ref, wq_ref, stage_ref, sem, *, layer):
    _side_cast(side_ref, side_out_ref)

    def attend(q):
        q = q.astype(BF16)
        scale = XATTN_HEAD_DIM ** -0.5
        for hd in range(XATTN_HEADS):
            cols = slice(hd * XATTN_HEAD_DIM, (hd + 1) * XATTN_HEAD_DIM)
            s = lax.dot_general(q[:, cols], k_ref[:, cols], (((1,), (1,)), ((), ())),
                                preferred_element_type=F32) * scale
            e = jnp.exp(s - jnp.max(s, axis=-1, keepdims=True))
            o = _dot(e.astype(BF16), v_ref[:, cols]) / jnp.sum(e, axis=-1, keepdims=True)
            o_ref[:, cols] = o.astype(o_ref.dtype)

    @pl.when(pl.program_id(0) == 0)
    def _():
        attend(_load_cast_matmul(h_ref, w_hbm, layer, wq_ref, stage_ref, sem))

    @pl.when(pl.program_id(0) > 0)
    def _():
        attend(_dot(h_ref[...], wq_ref[...]))


def _attention(h, w_q, k, v, side, layer, seq, mem_len, tm=512):
    t, d = h.shape
    n = t // tm
    tiles_per_seq = seq // tm
    row = pl.BlockSpec((tm, d), lambda i: (i, 0))
    mem_rows = pl.BlockSpec((mem_len, d), lambda i: (i // tiles_per_seq, 0))
    side_in, side_out, side_shape = _side_cast_specs(side, layer, n, col_tile=FFN_CHUNK)
    return pl.pallas_call(
        functools.partial(_attn_kernel, layer=layer),
        grid=(n,),
        in_specs=[row, mem_rows, mem_rows, HBM_SPEC, side_in],
        out_specs=[row, side_out],
        out_shape=[jax.ShapeDtypeStruct((t, d), BF16), side_shape],
        scratch_shapes=_weight_scratch(d, d),
        compiler_params=_params("arbitrary"),
        name="attention",
    )(h, k, v, w_q, side)


def _ffn_kernel(h_ref, gpost_ref, x_hbm, wu_hbm, wd_hbm, xo_ref, wu_buf, wd_buf, sem, x_sem, acc_ref, *,
                n_tiles):
    i = pl.program_id(0)
    n_chunks, _, tf = wu_hbm.shape
    tm = xo_ref.shape[0]

    def copies(f, slot):
        row0 = pl.multiple_of(f * tf, tf)
        return (pltpu.make_async_copy(wu_hbm.at[f], wu_buf.at[slot], sem.at[0, slot]),
                pltpu.make_async_copy(wd_hbm.at[pl.ds(row0, tf), :], wd_buf.at[slot], sem.at[1, slot]))

    def start(f, slot):
        for cp in copies(f, slot):
            cp.start()

    def wait(f, slot):
        for cp in copies(f, slot):
            cp.wait()

    @pl.when(i == 0)
    def _():
        start(0, 0)

    x_copy = pltpu.make_async_copy(x_hbm.at[pl.ds(pl.multiple_of(i * tm, tm), tm), :], xo_ref, x_sem.at[0])
    x_copy.start()

    def hidden(rows, slot):
        a = jnp.square(jnp.maximum(_dot(h_ref[rows, :], wu_buf[slot]), 0.0))
        return _dot(a.astype(BF16), wd_buf[slot])

    def finish(rows, acc):
        xo_ref[rows, :] = xo_ref[rows, :] + _rms(acc, gpost_ref[...])

    def pair(p, carry, first=False, last=False):
        for slot in (0, 1):
            f = 2 * p + slot
            start((f + 1) % n_chunks, 1 - slot)
            wait(f, slot)
            if last and slot == 1:
                x_copy.wait()
                block = tm // FFN_FINAL_ROW_BLOCKS
                for r in range(FFN_FINAL_ROW_BLOCKS):
                    rows = slice(r * block, (r + 1) * block)
                    finish(rows, acc_ref[rows, :] + hidden(rows, slot))
            elif first and slot == 0:
                acc_ref[...] = hidden(slice(None), slot)
            else:
                acc_ref[...] += hidden(slice(None), slot)
        return carry

    n_pairs = n_chunks // 2
    pair(0, 0, first=True)
    lax.fori_loop(1, n_pairs - 1, pair, 0)
    pair(n_pairs - 1, 0, last=True)

    @pl.when(i == n_tiles - 1)
    def _():
        wait(0, 0)


def _ffn(h, w_up, w_down, x, g_post, layer, tm=1024):
    t, d = h.shape
    n = t // tm
    tf = w_up.shape[-1]
    row = pl.BlockSpec((tm, d), lambda i: (i, 0))
    return pl.pallas_call(
        functools.partial(_ffn_kernel, n_tiles=n),
        grid=(n,),
        in_specs=[row, _layer_spec(layer, (1, d)), HBM_SPEC, HBM_SPEC, HBM_SPEC],
        out_specs=row,
        out_shape=jax.ShapeDtypeStruct((t, d), F32),
        scratch_shapes=[pltpu.VMEM((2, d, tf), BF16), pltpu.VMEM((2, tf, d), BF16),
                        pltpu.SemaphoreType.DMA((2, 2)), pltpu.SemaphoreType.DMA((1,)),
                        pltpu.VMEM((tm, d), F32)],
        compiler_params=_params("arbitrary"),
        name="ffn",
    )(h, g_post, x, w_up, w_down)


def kernel(x, mem, norm_mix_pre, norm_mix_post, w_in, w_out, gmlp_v_gain, w_spatial, b_spatial, w_pool,
           s_pool, w_dw, b_dw, conv_ln_g, conv_ln_b, norm_xattn_pre, norm_mem, norm_xattn_post, w_q, w_k,
           w_v, w_o, norm_ffn_pre, norm_ffn_post, w_up, w_down):
    batch, seq, d = x.shape
    mem_len = mem.shape[1]
    depth = w_in.shape[0]
    t = batch * seq

    vec = lambda a: a.reshape(depth, 1, -1)
    norm_mix_pre, norm_mix_post = vec(norm_mix_pre), vec(norm_mix_post)
    norm_xattn_pre, norm_mem, norm_xattn_post = vec(norm_xattn_pre), vec(norm_mem), vec(norm_xattn_post)
    norm_ffn_pre, norm_ffn_post = vec(norm_ffn_pre), vec(norm_ffn_post)
    g_v, s_pool, b_dw = vec(gmlp_v_gain), vec(s_pool), vec(b_dw)
    conv_ln_g, conv_ln_b = vec(conv_ln_g), vec(conv_ln_b)
    b_s_t = jnp.swapaxes(b_spatial, 1, 2)

    xf = x.reshape(t, d)
    memf = mem.reshape(batch * mem_len, d)

    for l in range(depth):
        u, vn, p, glu, w_down_bf = _mixer_in(xf, norm_mix_pre, w_in, g_v, w_down, l)
        y = _mixer_core(u, vn, p, glu, w_spatial, b_s_t, w_pool, s_pool, w_dw, b_dw, conv_ln_g, conv_ln_b,
                        l, seq)
        xf, h = _proj_res(y, w_out, xf, norm_mix_post, norm_xattn_pre, l)
        k, v = _kv_proj(memf, norm_mem, w_k, w_v, l)
        a, w_up_bf = _attention(h, w_q, k, v, w_up, l, seq, mem_len)
        xf, h = _proj_res(a, w_o, xf, norm_xattn_post, norm_ffn_pre, l)
        xf = _ffn(h, w_up_bf, w_down_bf, xf, norm_ffn_post, l)
    return xf.reshape(batch, seq, d)
```

```python
import functools

import jax
import jax.numpy as jnp
from jax import lax
from jax.experimental import pallas as pl
from jax.experimental.pallas import tpu as pltpu

BF16 = jnp.bfloat16
F32 = jnp.float32

D_MODEL = 2048
HEAD_DIM = 128
CHUNK = 128
GMLP_WIDTH = 1024
GMLP_HEADS = GMLP_WIDTH // HEAD_DIM
POOL_WIDTH = 512
POOL_WINDOWS = (2, 4, 8, 16)
POOL_GROUP_WIDTH = POOL_WIDTH // len(POOL_WINDOWS)
CONV_WIDTH = 512
CONV_K = 31
IN_COLS = 2 * GMLP_WIDTH + POOL_WIDTH + 2 * CONV_WIDTH
XATTN_HEADS = 4
XATTN_HEAD_DIM = D_MODEL // XATTN_HEADS
RMS_EPS = 1e-6
LN_EPS = 1e-5

HALO = 32
SUBLANES = 8
VMEM_LIMIT_BYTES = 56 * 1024 * 1024
PROJ_ROW_CHUNKS = 4
FFN_CHUNK = 512
FFN_FINAL_ROW_BLOCKS = 4
WEIGHT_CHUNK_ROWS = 256


def _params(*semantics):
    return pltpu.CompilerParams(dimension_semantics=semantics, vmem_limit_bytes=VMEM_LIMIT_BYTES)


def _rms(x, g):
    return x * lax.rsqrt(jnp.mean(x * x, axis=-1, keepdims=True) + RMS_EPS) * g


def _layer_norm(x, g, b=None):
    mu = jnp.mean(x, axis=-1, keepdims=True)
    xc = x - mu
    var = jnp.mean(xc * xc, axis=-1, keepdims=True)
    y = xc * lax.rsqrt(var + LN_EPS) * g
    if b is not None:
        y = y + b
    return y


def _dot(a, b):
    return jnp.dot(a, b, preferred_element_type=F32)


def _layer_spec(layer, shape):
    zeros = (0,) * len(shape)
    return pl.BlockSpec((None,) + tuple(shape), lambda *_: (layer,) + zeros)


def _load_cast_weight(w_hbm, layer, wbf_ref, stage_ref, sem):
    chunk_rows = stage_ref.shape[1]
    n_chunks = wbf_ref.shape[0] // chunk_rows

    def copy(c):
        return pltpu.make_async_copy(w_hbm.at[layer, pl.ds(c * chunk_rows, chunk_rows), :],
                                     stage_ref.at[c % 2], sem.at[c % 2])

    copy(0).start()
    for c in range(n_chunks):
        if c + 1 < n_chunks:
            copy(c + 1).start()
        copy(c).wait()
        wbf_ref[c * chunk_rows:(c + 1) * chunk_rows, :] = stage_ref[c % 2].astype(BF16)


def _load_cast_matmul(a_ref, w_hbm, layer, wbf_ref, stage_ref, sem):
    chunk_rows = stage_ref.shape[1]
    n_chunks = wbf_ref.shape[0] // chunk_rows

    def copy(c):
        return pltpu.make_async_copy(w_hbm.at[layer, pl.ds(c * chunk_rows, chunk_rows), :],
                                     stage_ref.at[c % 2], sem.at[c % 2])

    copy(0).start()
    acc = None
    for c in range(n_chunks):
        if c + 1 < n_chunks:
            copy(c + 1).start()
        copy(c).wait()
        rows = slice(c * chunk_rows, (c + 1) * chunk_rows)
        wbf_ref[rows, :] = stage_ref[c % 2].astype(BF16)
        part = _dot(a_ref[:, rows], wbf_ref[rows, :])
        acc = part if acc is None else acc + part
    return acc


def _weight_scratch(k, n, chunk_rows=WEIGHT_CHUNK_ROWS):
    return [pltpu.VMEM((k, n), BF16), pltpu.VMEM((2, chunk_rows, n), F32), pltpu.SemaphoreType.DMA((2,))]


HBM_SPEC = pl.BlockSpec(memory_space=pl.ANY)


def _side_cast_specs(w, layer, n_steps, col_tile=None):
    _, rows, cols = w.shape
    rs = rows // n_steps
    in_spec = pl.BlockSpec((None, rs, cols), lambda i: (layer, i, 0))
    if col_tile is None:
        return in_spec, pl.BlockSpec((rs, cols), lambda i: (i, 0)), jax.ShapeDtypeStruct((rows, cols), BF16)
    n_tiles = cols // col_tile
    out_spec = pl.BlockSpec((n_tiles, rs, col_tile), lambda i: (0, i, 0))
    return in_spec, out_spec, jax.ShapeDtypeStruct((n_tiles, rows, col_tile), BF16)


def _side_cast(side_ref, side_out_ref):
    if len(side_out_ref.shape) == 2:
        side_out_ref[...] = side_ref[...].astype(BF16)
    else:
        n_tiles, _, col_tile = side_out_ref.shape
        for j in range(n_tiles):
            side_out_ref[j] = side_ref[:, j * col_tile:(j + 1) * col_tile].astype(BF16)


def _mixer_in_kernel(x_ref, gpre_ref, gv_ref, w_hbm, side_ref, u_ref, vn_ref, p_ref, glu_ref, side_out_ref,
                     w_ref, stage_ref, sem, *, layer):
    _side_cast(side_ref, side_out_ref)

    @pl.when(pl.program_id(0) == 0)
    def _():
        _load_cast_weight(w_hbm, layer, w_ref, stage_ref, sem)

    h = _rms(x_ref[...], gpre_ref[...]).astype(BF16)
    u_ref[...] = jax.nn.gelu(_dot(h, w_ref[:, 0:GMLP_WIDTH])).astype(u_ref.dtype)
    v = jax.nn.gelu(_dot(h, w_ref[:, GMLP_WIDTH:2 * GMLP_WIDTH]))
    for hd in range(GMLP_HEADS):
        sl = slice(hd * HEAD_DIM, (hd + 1) * HEAD_DIM)
        vn_ref[:, sl] = _layer_norm(v[:, sl], gv_ref[:, sl]).astype(vn_ref.dtype)
    c0 = 2 * GMLP_WIDTH
    p_ref[...] = _dot(h, w_ref[:, c0:c0 + POOL_WIDTH])
    c1 = c0 + POOL_WIDTH
    c_val = _dot(h, w_ref[:, c1:c1 + CONV_WIDTH])
    c_gate = _dot(h, w_ref[:, c1 + CONV_WIDTH:c1 + 2 * CONV_WIDTH])
    glu_ref[...] = c_val * jax.nn.sigmoid(c_gate)


def _mixer_in(x, g_pre, w_in, g_v, side, layer, tm=512):
    t, d = x.shape
    n = t // tm
    row = lambda c: pl.BlockSpec((tm, c), lambda i: (i, 0))
    side_in, side_out, side_shape = _side_cast_specs(side, layer, n)
    return pl.pallas_call(
        functools.partial(_mixer_in_kernel, layer=layer),
        grid=(n,),
        in_specs=[row(d), _layer_spec(layer, (1, d)), _layer_spec(layer, (1, GMLP_WIDTH)), HBM_SPEC, side_in],
        out_specs=[row(GMLP_WIDTH), row(GMLP_WIDTH), row(POOL_WIDTH), row(CONV_WIDTH), side_out],
        out_shape=[
            jax.ShapeDtypeStruct((t, GMLP_WIDTH), BF16),
            jax.ShapeDtypeStruct((t, GMLP_WIDTH), BF16),
            jax.ShapeDtypeStruct((t, POOL_WIDTH), F32),
            jax.ShapeDtypeStruct((t, CONV_WIDTH), F32),
            side_shape,
        ],
        scratch_shapes=_weight_scratch(d, IN_COLS, chunk_rows=128),
        compiler_params=_params("arbitrary"),
        name="mixer_in",
    )(x, g_pre, g_v, w_in, side)


def _mixer_chunk(c, tile, u_ref, vn_ref, p_ref, ws_ref, bst_ref, wp_ref, sp_ref, bdw_ref, lng_ref, lnb_ref,
                 y_ref, pext_ref, cext_ref, shift_ref, wtap_ref, conv_ref, *, tm, tiles_per_seq, conv_rows):
    row0 = c * CHUNK
    rows = slice(row0, row0 + CHUNK)

    tri = (lax.broadcasted_iota(jnp.int32, (CHUNK, CHUNK), 0)
           >= lax.broadcasted_iota(jnp.int32, (CHUNK, CHUNK), 1))
    for hd in range(GMLP_HEADS):
        w_mask = jnp.where(tri, ws_ref[hd], 0.0).astype(BF16)
        cols = slice(hd * HEAD_DIM, (hd + 1) * HEAD_DIM)
        mixed = _dot(w_mask, vn_ref[rows, cols]) + bst_ref[:, hd:hd + 1]
        y_ref[rows, cols] = (u_ref[rows, cols].astype(F32) * mixed).astype(y_ref.dtype)

    pos = (lax.broadcasted_iota(jnp.int32, (CHUNK, 1), 0) + (tile % tiles_per_seq) * tm + row0).astype(F32)
    for g, w in enumerate(POOL_WINDOWS):
        cols = slice(g * POOL_GROUP_WIDTH, (g + 1) * POOL_GROUP_WIDTH)
        win = pext_ref[HALO + row0 - (w - 1):HALO + row0 + CHUNK, cols]
        span = 1
        while span < w:
            win = win[span:, :] + win[:-span, :]
            span *= 2
        pooled = win / jnp.minimum(pos + 1.0, float(w)) - p_ref[rows, cols]
        out = _dot(pooled.astype(BF16), wp_ref[g].astype(BF16)) * sp_ref[:, cols]
        y_ref[rows, GMLP_WIDTH + g * POOL_GROUP_WIDTH:GMLP_WIDTH + (g + 1) * POOL_GROUP_WIDTH] = (
            out.astype(y_ref.dtype))

    n_shift_rows = CHUNK + HALO - SUBLANES
    for s in range(1, SUBLANES):
        shift_ref[s - 1, 0:n_shift_rows, :] = cext_ref[row0 + s:row0 + s + n_shift_rows, :]
    base = HALO - (CONV_K - 1)
    bias = jnp.broadcast_to(bdw_ref[...], (SUBLANES, CONV_WIDTH))
    for r in range(CHUNK // conv_rows):
        acc = [bias] * (conv_rows // SUBLANES)
        for j in range(CONV_K):
            s, start = (base + j) % SUBLANES, r * conv_rows + (base + j) // SUBLANES * SUBLANES
            w_tap = wtap_ref[j]
            for q in range(conv_rows // SUBLANES):
                lo = start + q * SUBLANES
                if s == 0:
                    tap = cext_ref[row0 + lo:row0 + lo + SUBLANES, :]
                else:
                    tap = shift_ref[s - 1, lo:lo + SUBLANES, :]
                acc[q] = acc[q] + w_tap * tap
        for q in range(conv_rows // SUBLANES):
            lo = r * conv_rows + q * SUBLANES
            conv_ref[lo:lo + SUBLANES, :] = acc[q]
    yc = jax.nn.silu(_layer_norm(conv_ref[...], lng_ref[...], lnb_ref[...]))
    y_ref[rows, GMLP_WIDTH + POOL_WIDTH:] = yc.astype(y_ref.dtype)


def _mixer_core_kernel(u_ref, vn_ref, p_ref, ph_ref, c_ref, ch_ref, ws_ref, bst_ref, wp_ref, sp_ref, wdw_ref,
                       bdw_ref, lng_ref, lnb_ref, y_ref, pext_ref, cext_ref, shift_ref, wtap_ref, conv_ref,
                       *, tm, tiles_per_seq):
    tile = pl.program_id(0)
    for j in range(CONV_K):
        wtap_ref[j] = jnp.broadcast_to(wdw_ref[j:j + 1, :], (SUBLANES, CONV_WIDTH))

    keep = jnp.where(tile % tiles_per_seq == 0, 0.0, 1.0).astype(F32)
    pext_ref[0:HALO, :] = ph_ref[...] * keep
    pext_ref[HALO:HALO + tm, :] = p_ref[...]
    cext_ref[0:HALO, :] = ch_ref[...] * keep
    cext_ref[HALO:HALO + tm, :] = c_ref[...]

    for c in range(tm // CHUNK):
        _mixer_chunk(c, tile, u_ref, vn_ref, p_ref, ws_ref, bst_ref, wp_ref, sp_ref, bdw_ref, lng_ref, lnb_ref,
                     y_ref, pext_ref, cext_ref, shift_ref, wtap_ref, conv_ref,
                     tm=tm, tiles_per_seq=tiles_per_seq, conv_rows=32)


def _mixer_core(u, vn, p, glu, w_s, b_s_t, w_pool, s_pool, w_dw, b_dw, ln_g, ln_b, layer, seq, tm=512):
    t = u.shape[0]
    row = lambda n: pl.BlockSpec((tm, n), lambda i: (i, 0))
    halo = lambda n: pl.BlockSpec((HALO, n), lambda i: (jnp.maximum(i * (tm // HALO) - 1, 0), 0))
    kernel = functools.partial(_mixer_core_kernel, tm=tm, tiles_per_seq=seq // tm)
    return pl.pallas_call(
        kernel,
        grid=(t // tm,),
        in_specs=[
            row(GMLP_WIDTH), row(GMLP_WIDTH),
            row(POOL_WIDTH), halo(POOL_WIDTH),
            row(CONV_WIDTH), halo(CONV_WIDTH),
            _layer_spec(layer, (GMLP_HEADS, CHUNK, CHUNK)),
            _layer_spec(layer, (CHUNK, GMLP_HEADS)),
            _layer_spec(layer, (len(POOL_WINDOWS), POOL_GROUP_WIDTH, POOL_GROUP_WIDTH)),
            _layer_spec(layer, (1, POOL_WIDTH)),
            _layer_spec(layer, (CONV_K, CONV_WIDTH)),
            _layer_spec(layer, (1, CONV_WIDTH)),
            _layer_spec(layer, (1, CONV_WIDTH)),
            _layer_spec(layer, (1, CONV_WIDTH)),
        ],
        out_specs=row(D_MODEL),
        out_shape=jax.ShapeDtypeStruct((t, D_MODEL), BF16),
        scratch_shapes=[
            pltpu.VMEM((HALO + tm, POOL_WIDTH), F32),
            pltpu.VMEM((HALO + tm, CONV_WIDTH), F32),
            pltpu.VMEM((SUBLANES - 1, HALO + CHUNK, CONV_WIDTH), F32),
            pltpu.VMEM((CONV_K, SUBLANES, CONV_WIDTH), F32),
            pltpu.VMEM((CHUNK, CONV_WIDTH), F32),
        ],
        compiler_params=_params("parallel"),
        name="mixer_core",
    )(u, vn, p, p, glu, glu, w_s, b_s_t, w_pool, s_pool, w_dw, b_dw, ln_g, ln_b)


def _proj_res_kernel(a_ref, x_ref, gpost_ref, gnext_ref, w_hbm, xo_ref, ho_ref, w_ref, stage_ref, sem, *,
                     layer):
    def finish(rows, h):
        xn = x_ref[rows, :] + _rms(h, gpost_ref[...])
        xo_ref[rows, :] = xn
        ho_ref[rows, :] = _rms(xn, gnext_ref[...]).astype(ho_ref.dtype)

    @pl.when(pl.program_id(0) == 0)
    def _():
        finish(slice(None), _load_cast_matmul(a_ref, w_hbm, layer, w_ref, stage_ref, sem))

    @pl.when(pl.program_id(0) > 0)
    def _():
        chunk = a_ref.shape[0] // PROJ_ROW_CHUNKS
        for c in range(PROJ_ROW_CHUNKS):
            rows = slice(c * chunk, (c + 1) * chunk)
            finish(rows, _dot(a_ref[rows, :], w_ref[...]))


def _proj_res(a, w, x, g_post, g_next, layer, tm=512):
    t, k = a.shape
    d = x.shape[1]
    row = lambda c: pl.BlockSpec((tm, c), lambda i: (i, 0))
    return pl.pallas_call(
        functools.partial(_proj_res_kernel, layer=layer),
        grid=(t // tm,),
        in_specs=[row(k), row(d), _layer_spec(layer, (1, d)), _layer_spec(layer, (1, d)), HBM_SPEC],
        out_specs=[row(d), row(d)],
        out_shape=[jax.ShapeDtypeStruct((t, d), F32), jax.ShapeDtypeStruct((t, d), BF16)],
        scratch_shapes=_weight_scratch(k, d),
        compiler_params=_params("arbitrary"),
        name="proj_res",
    )(a, x, g_post, g_next, w)


def _kv_kernel(m_ref, g_ref, wk_ref, wv_ref, k_ref, v_ref, mn_ref):
    @pl.when(pl.program_id(1) == 0)
    def _():
        mn_ref[...] = _rms(m_ref[...], g_ref[...]).astype(BF16)

    k_ref[...] = _dot(mn_ref[...], wk_ref[...].astype(BF16)).astype(k_ref.dtype)
    v_ref[...] = _dot(mn_ref[...], wv_ref[...].astype(BF16)).astype(v_ref.dtype)


def _kv_proj(mem, g_mem, w_k, w_v, tn=256):
    t, d = mem.shape
    depth = w_k.shape[0]
    full = pl.BlockSpec((t, d), lambda l, j: (0, 0))
    gain = pl.BlockSpec((None, 1, d), lambda l, j: (l, 0, 0))
    wcol = pl.BlockSpec((None, d, tn), lambda l, j: (l, 0, j))
    ocol = pl.BlockSpec((None, t, tn), lambda l, j: (l, 0, j))
    return pl.pallas_call(
        _kv_kernel,
        grid=(depth, d // tn),
        in_specs=[full, gain, wcol, wcol],
        out_specs=[ocol, ocol],
        out_shape=[jax.ShapeDtypeStruct((depth, t, d), BF16)] * 2,
        scratch_shapes=[pltpu.VMEM((t, d), BF16)],
        compiler_params=_params("arbitrary", "arbitrary"),
        name="kv_proj",
    )(mem, g_mem, w_k, w_v)


def _attn_kernel(h_ref, k_ref, v_ref, w_hbm, side_ref, o_ref, side_out_ref, wq_ref, stage_ref, sem, *, layer):
    _side_cast(side_ref, side_out_ref)

    def attend(q):
        q = q.astype(BF16)
        scale = XATTN_HEAD_DIM ** -0.5
        for hd in range(XATTN_HEADS):
            cols = slice(hd * XATTN_HEAD_DIM, (hd + 1) * XATTN_HEAD_DIM)
            s = lax.dot_general(q[:, cols], k_ref[:, cols], (((1,), (1,)), ((), ())),
                                preferred_element_type=F32) * scale
            e = jnp.exp(s - jnp.max(s, axis=-1, keepdims=True))
            o = _dot(e.astype(BF16), v_ref[:, cols]) / jnp.sum(e, axis=-1, keepdims=True)
            o_ref[:, cols] = o.astype(o_ref.dtype)

    @pl.when(pl.program_id(0) == 0)
    def _():
        attend(_load_cast_matmul(h_ref, w_hbm, layer, wq_ref, stage_ref, sem))

    @pl.when(pl.program_id(0) > 0)
    def _():
        attend(_dot(h_ref[...], wq_ref[...]))


def _attention(h, w_q, k, v, side, layer, seq, mem_len, tm=512):
    t, d = h.shape
    n = t // tm
    tiles_per_seq = seq // tm
    row = pl.BlockSpec((tm, d), lambda i: (i, 0))
    mem_rows = pl.BlockSpec((None, mem_len, d), lambda i: (layer, i // tiles_per_seq, 0))
    side_in, side_out, side_shape = _side_cast_specs(side, layer, n, col_tile=FFN_CHUNK)
    return pl.pallas_call(
        functools.partial(_attn_kernel, layer=layer),
        grid=(n,),
        in_specs=[row, mem_rows, mem_rows, HBM_SPEC, side_in],
        out_specs=[row, side_out],
        out_shape=[jax.ShapeDtypeStruct((t, d), BF16), side_shape],
        scratch_shapes=_weight_scratch(d, d),
        compiler_params=_params("arbitrary"),
        name="attention",
    )(h, k, v, w_q, side)


def _ffn_kernel(h_ref, gpost_ref, x_hbm, wu_hbm, wd_hbm, xo_ref, wu_buf, wd_buf, sem, x_sem, acc_ref, *,
                n_tiles):
    i = pl.program_id(0)
    n_chunks, _, tf = wu_hbm.shape
    tm = xo_ref.shape[0]

    def copies(f, slot):
        row0 = pl.multiple_of(f * tf, tf)
        return (pltpu.make_async_copy(wu_hbm.at[f], wu_buf.at[slot], sem.at[0, slot]),
                pltpu.make_async_copy(wd_hbm.at[pl.ds(row0, tf), :], wd_buf.at[slot], sem.at[1, slot]))

    def start(f, slot):
        for cp in copies(f, slot):
            cp.start()

    def wait(f, slot):
        for cp in copies(f, slot):
            cp.wait()

    @pl.when(i == 0)
    def _():
        start(0, 0)

    x_copy = pltpu.make_async_copy(x_hbm.at[pl.ds(pl.multiple_of(i * tm, tm), tm), :], xo_ref, x_sem.at[0])
    x_copy.start()

    def hidden(rows, slot):
        a = jnp.square(jnp.maximum(_dot(h_ref[rows, :], wu_buf[slot]), 0.0))
        return _dot(a.astype(BF16), wd_buf[slot])

    def finish(rows, acc):
        xo_ref[rows, :] = xo_ref[rows, :] + _rms(acc, gpost_ref[...])

    def pair(p, carry, first=False, last=False):
        for slot in (0, 1):
            f = 2 * p + slot
            start((f + 1) % n_chunks, 1 - slot)
            wait(f, slot)
            if last and slot == 1:
                x_copy.wait()
                block = tm // FFN_FINAL_ROW_BLOCKS
                for r in range(FFN_FINAL_ROW_BLOCKS):
                    rows = slice(r * block, (r + 1) * block)
                    finish(rows, acc_ref[rows, :] + hidden(rows, slot))
            elif first and slot == 0:
                acc_ref[...] = hidden(slice(None), slot)
            else:
                acc_ref[...] += hidden(slice(None), slot)
        return carry

    n_pairs = n_chunks // 2
    pair(0, 0, first=True)
    lax.fori_loop(1, n_pairs - 1, pair, 0)
    pair(n_pairs - 1, 0, last=True)

    @pl.when(i == n_tiles - 1)
    def _():
        wait(0, 0)


def _ffn(h, w_up, w_down, x, g_post, layer, tm=1024):
    t, d = h.shape
    n = t // tm
    tf = w_up.shape[-1]
    row = pl.BlockSpec((tm, d), lambda i: (i, 0))
    return pl.pallas_call(
        functools.partial(_ffn_kernel, n_tiles=n),
        grid=(n,),
        in_specs=[row, _layer_spec(layer, (1, d)), HBM_SPEC, HBM_SPEC, HBM_SPEC],
        out_specs=row,
        out_shape=jax.ShapeDtypeStruct((t, d), F32),
        scratch_shapes=[pltpu.VMEM((2, d, tf), BF16), pltpu.VMEM((2, tf, d), BF16),
                        pltpu.SemaphoreType.DMA((2, 2)), pltpu.SemaphoreType.DMA((1,)),
                        pltpu.VMEM((tm, d), F32)],
        compiler_params=_params("arbitrary"),
        name="ffn",
    )(h, g_post, x, w_up, w_down)


def kernel(x, mem, norm_mix_pre, norm_mix_post, w_in, w_out, gmlp_v_gain, w_spatial, b_spatial, w_pool,
           s_pool, w_dw, b_dw, conv_ln_g, conv_ln_b, norm_xattn_pre, norm_mem, norm_xattn_post, w_q, w_k,
           w_v, w_o, norm_ffn_pre, norm_ffn_post, w_up, w_down):
    batch, seq, d = x.shape
    mem_len = mem.shape[1]
    depth = w_in.shape[0]
    t = batch * seq

    vec = lambda a: a.reshape(depth, 1, -1)
    norm_mix_pre, norm_mix_post = vec(norm_mix_pre), vec(norm_mix_post)
    norm_xattn_pre, norm_mem, norm_xattn_post = vec(norm_xattn_pre), vec(norm_mem), vec(norm_xattn_post)
    norm_ffn_pre, norm_ffn_post = vec(norm_ffn_pre), vec(norm_ffn_post)
    g_v, s_pool, b_dw = vec(gmlp_v_gain), vec(s_pool), vec(b_dw)
    conv_ln_g, conv_ln_b = vec(conv_ln_g), vec(conv_ln_b)
    b_s_t = jnp.swapaxes(b_spatial, 1, 2)

    xf = x.reshape(t, d)
    memf = mem.reshape(batch * mem_len, d)

    k, v = _kv_proj(memf, norm_mem, w_k, w_v)
    for l in range(depth):
        u, vn, p, glu, w_down_bf = _mixer_in(xf, norm_mix_pre, w_in, g_v, w_down, l)
        y = _mixer_core(u, vn, p, glu, w_spatial, b_s_t, w_pool, s_pool, w_dw, b_dw, conv_ln_g, conv_ln_b,
                        l, seq)
        xf, h = _proj_res(y, w_out, xf, norm_mix_post, norm_xattn_pre, l)
        a, w_up_bf = _attention(h, w_q, k, v, w_up, l, seq, mem_len)
        xf, h = _proj_res(a, w_o, xf, norm_xattn_post, norm_ffn_pre, l)
        xf = _ffn(h, w_up_bf, w_down_bf, xf, norm_ffn_post, l)
    return xf.reshape(batch, seq, d)
```

```python
import functools

import jax
import jax.numpy as jnp
from jax import lax
from jax.experimental import pallas as pl
from jax.experimental.pallas import tpu as pltpu

BF16 = jnp.bfloat16
F32 = jnp.float32

D_MODEL = 2048
HEAD_DIM = 128
CHUNK = 128
GMLP_WIDTH = 1024
GMLP_HEADS = GMLP_WIDTH // HEAD_DIM
POOL_WIDTH = 512
POOL_WINDOWS = (2, 4, 8, 16)
POOL_GROUP_WIDTH = POOL_WIDTH // len(POOL_WINDOWS)
CONV_WIDTH = 512
CONV_K = 31
IN_COLS = 2 * GMLP_WIDTH + POOL_WIDTH + 2 * CONV_WIDTH
XATTN_HEADS = 4
XATTN_HEAD_DIM = D_MODEL // XATTN_HEADS
RMS_EPS = 1e-6
LN_EPS = 1e-5

HALO = 32
SUBLANES = 8
VMEM_LIMIT_BYTES = 56 * 1024 * 1024
PROJ_ROW_CHUNKS = 4
FFN_CHUNK = 512
FFN_FINAL_ROW_BLOCKS = 4
WEIGHT_CHUNK_ROWS = 256


def _params(*semantics):
    return pltpu.CompilerParams(dimension_semantics=semantics, vmem_limit_bytes=VMEM_LIMIT_BYTES)


def _rms(x, g):
    return x * lax.rsqrt(jnp.mean(x * x, axis=-1, keepdims=True) + RMS_EPS) * g


def _layer_norm(x, g, b=None):
    mu = jnp.mean(x, axis=-1, keepdims=True)
    xc = x - mu
    var = jnp.mean(xc * xc, axis=-1, keepdims=True)
    y = xc * lax.rsqrt(var + LN_EPS) * g
    if b is not None:
        y = y + b
    return y


def _dot(a, b):
    return jnp.dot(a, b, preferred_element_type=F32)


def _layer_spec(layer, shape):
    zeros = (0,) * len(shape)
    return pl.BlockSpec((None,) + tuple(shape), lambda *_: (layer,) + zeros)


def _load_cast_weight(w_hbm, layer, wbf_ref, stage_ref, sem):
    chunk_rows = stage_ref.shape[1]
    n_chunks = wbf_ref.shape[0] // chunk_rows

    def copy(c):
        return pltpu.make_async_copy(w_hbm.at[layer, pl.ds(c * chunk_rows, chunk_rows), :],
                                     stage_ref.at[c % 2], sem.at[c % 2])

    copy(0).start()
    for c in range(n_chunks):
        if c + 1 < n_chunks:
            copy(c + 1).start()
        copy(c).wait()
        wbf_ref[c * chunk_rows:(c + 1) * chunk_rows, :] = stage_ref[c % 2].astype(BF16)


def _load_cast_matmul(a_ref, w_hbm, layer, wbf_ref, stage_ref, sem):
    chunk_rows = stage_ref.shape[1]
    n_chunks = wbf_ref.shape[0] // chunk_rows

    def copy(c):
        return pltpu.make_async_copy(w_hbm.at[layer, pl.ds(c * chunk_rows, chunk_rows), :],
                                     stage_ref.at[c % 2], sem.at[c % 2])

    copy(0).start()
    acc = None
    for c in range(n_chunks):
        if c + 1 < n_chunks:
            copy(c + 1).start()
        copy(c).wait()
        rows = slice(c * chunk_rows, (c + 1) * chunk_rows)
        wbf_ref[rows, :] = stage_ref[c % 2].astype(BF16)
        part = _dot(a_ref[:, rows], wbf_ref[rows, :])
        acc = part if acc is None else acc + part
    return acc


def _weight_scratch(k, n, chunk_rows=WEIGHT_CHUNK_ROWS):
    return [pltpu.VMEM((k, n), BF16), pltpu.VMEM((2, chunk_rows, n), F32), pltpu.SemaphoreType.DMA((2,))]


HBM_SPEC = pl.BlockSpec(memory_space=pl.ANY)


def _side_cast_specs(w, layer, n_steps, col_tile=None):
    _, rows, cols = w.shape
    rs = rows // n_steps
    in_spec = pl.BlockSpec((None, rs, cols), lambda i: (layer, i, 0))
    if col_tile is None:
        return in_spec, pl.BlockSpec((rs, cols), lambda i: (i, 0)), jax.ShapeDtypeStruct((rows, cols), BF16)
    n_tiles = cols // col_tile
    out_spec = pl.BlockSpec((n_tiles, rs, col_tile), lambda i: (0, i, 0))
    return in_spec, out_spec, jax.ShapeDtypeStruct((n_tiles, rows, col_tile), BF16)


def _side_cast(side_ref, side_out_ref):
    if len(side_out_ref.shape) == 2:
        side_out_ref[...] = side_ref[...].astype(BF16)
    else:
        n_tiles, _, col_tile = side_out_ref.shape
        for j in range(n_tiles):
            side_out_ref[j] = side_ref[:, j * col_tile:(j + 1) * col_tile].astype(BF16)


def _mixer_in_kernel(x_ref, gpre_ref, gv_ref, w_hbm, side_ref, u_ref, vn_ref, p_ref, glu_ref, side_out_ref,
                     w_ref, stage_ref, sem, *, layer):
    _side_cast(side_ref, side_out_ref)

    @pl.when(pl.program_id(0) == 0)
    def _():
        _load_cast_weight(w_hbm, layer, w_ref, stage_ref, sem)

    h = _rms(x_ref[...], gpre_ref[...]).astype(BF16)
    u_ref[...] = jax.nn.gelu(_dot(h, w_ref[:, 0:GMLP_WIDTH])).astype(u_ref.dtype)
    v = jax.nn.gelu(_dot(h, w_ref[:, GMLP_WIDTH:2 * GMLP_WIDTH]))
    for hd in range(GMLP_HEADS):
        sl = slice(hd * HEAD_DIM, (hd + 1) * HEAD_DIM)
        vn_ref[:, sl] = _layer_norm(v[:, sl], gv_ref[:, sl]).astype(vn_ref.dtype)
    c0 = 2 * GMLP_WIDTH
    p_ref[...] = _dot(h, w_ref[:, c0:c0 + POOL_WIDTH])
    c1 = c0 + POOL_WIDTH
    c_val = _dot(h, w_ref[:, c1:c1 + CONV_WIDTH])
    c_gate = _dot(h, w_ref[:, c1 + CONV_WIDTH:c1 + 2 * CONV_WIDTH])
    glu_ref[...] = c_val * jax.nn.sigmoid(c_gate)


def _mixer_in(x, g_pre, w_in, g_v, side, layer, tm=512):
    t, d = x.shape
    n = t // tm
    row = lambda c: pl.BlockSpec((tm, c), lambda i: (i, 0))
    side_in, side_out, side_shape = _side_cast_specs(side, layer, n)
    return pl.pallas_call(
        functools.partial(_mixer_in_kernel, layer=layer),
        grid=(n,),
        in_specs=[row(d), _layer_spec(layer, (1, d)), _layer_spec(layer, (1, GMLP_WIDTH)), HBM_SPEC, side_in],
        out_specs=[row(GMLP_WIDTH), row(GMLP_WIDTH), row(POOL_WIDTH), row(CONV_WIDTH), side_out],
        out_shape=[
            jax.ShapeDtypeStruct((t, GMLP_WIDTH), BF16),
            jax.ShapeDtypeStruct((t, GMLP_WIDTH), BF16),
            jax.ShapeDtypeStruct((t, POOL_WIDTH), F32),
            jax.ShapeDtypeStruct((t, CONV_WIDTH), F32),
            side_shape,
        ],
        scratch_shapes=_weight_scratch(d, IN_COLS, chunk_rows=128),
        compiler_params=_params("arbitrary"),
        name="mixer_in",
    )(x, g_pre, g_v, w_in, side)


def _mixer_chunk(c, tile, u_ref, vn_ref, p_ref, ws_ref, bst_ref, wp_ref, sp_ref, bdw_ref, lng_ref, lnb_ref,
                 y_ref, pext_ref, cext_ref, shift_ref, wtap_ref, conv_ref, *, tm, tiles_per_seq, conv_rows):
    row0 = c * CHUNK
    rows = slice(row0, row0 + CHUNK)

    tri = (lax.broadcasted_iota(jnp.int32, (CHUNK, CHUNK), 0)
           >= lax.broadcasted_iota(jnp.int32, (CHUNK, CHUNK), 1))
    for hd in range(GMLP_HEADS):
        w_mask = jnp.where(tri, ws_ref[hd], 0.0).astype(BF16)
        cols = slice(hd * HEAD_DIM, (hd + 1) * HEAD_DIM)
        mixed = _dot(w_mask, vn_ref[rows, cols]) + bst_ref[:, hd:hd + 1]
        y_ref[rows, cols] = (u_ref[rows, cols].astype(F32) * mixed).astype(y_ref.dtype)

    pos = (lax.broadcasted_iota(jnp.int32, (CHUNK, 1), 0) + (tile % tiles_per_seq) * tm + row0).astype(F32)
    for g, w in enumerate(POOL_WINDOWS):
        cols = slice(g * POOL_GROUP_WIDTH, (g + 1) * POOL_GROUP_WIDTH)
        win = pext_ref[HALO + row0 - (w - 1):HALO + row0 + CHUNK, cols]
        span = 1
        while span < w:
            win = win[span:, :] + win[:-span, :]
            span *= 2
        pooled = win / jnp.minimum(pos + 1.0, float(w)) - p_ref[rows, cols]
        out = _dot(pooled.astype(BF16), wp_ref[g].astype(BF16)) * sp_ref[:, cols]
        y_ref[rows, GMLP_WIDTH + g * POOL_GROUP_WIDTH:GMLP_WIDTH + (g + 1) * POOL_GROUP_WIDTH] = (
            out.astype(y_ref.dtype))

    n_shift_rows = CHUNK + HALO - SUBLANES
    for s in range(1, SUBLANES):
        shift_ref[s - 1, 0:n_shift_rows, :] = cext_ref[row0 + s:row0 + s + n_shift_rows, :]
    base = HALO - (CONV_K - 1)
    bias = jnp.broadcast_to(bdw_ref[...], (SUBLANES, CONV_WIDTH))
    for r in range(CHUNK // conv_rows):
        acc = [bias] * (conv_rows // SUBLANES)
        for j in range(CONV_K):
            s, start = (base + j) % SUBLANES, r * conv_rows + (base + j) // SUBLANES * SUBLANES
            w_tap = wtap_ref[j]
            for q in range(conv_rows // SUBLANES):
                lo = start + q * SUBLANES
                if s == 0:
                    tap = cext_ref[row0 + lo:row0 + lo + SUBLANES, :]
                else:
                    tap = shift_ref[s - 1, lo:lo + SUBLANES, :]
                acc[q] = acc[q] + w_tap * tap
        for q in range(conv_rows // SUBLANES):
            lo = r * conv_rows + q * SUBLANES
            conv_ref[lo:lo + SUBLANES, :] = acc[q]
    yc = jax.nn.silu(_layer_norm(conv_ref[...], lng_ref[...], lnb_ref[...]))
    y_ref[rows, GMLP_WIDTH + POOL_WIDTH:] = yc.astype(y_ref.dtype)


def _mixer_core_kernel(u_ref, vn_ref, p_ref, ph_ref, c_ref, ch_ref, ws_ref, bst_ref, wp_ref, sp_ref, wdw_ref,
                       bdw_ref, lng_ref, lnb_ref, side_ref, y_ref, side_out_ref, pext_ref, cext_ref, shift_ref,
                       wtap_ref, conv_ref, *, tm, tiles_per_seq):
    tile = pl.program_id(0)
    _side_cast(side_ref, side_out_ref)
    for j in range(CONV_K):
        wtap_ref[j] = jnp.broadcast_to(wdw_ref[j:j + 1, :], (SUBLANES, CONV_WIDTH))

    keep = jnp.where(tile % tiles_per_seq == 0, 0.0, 1.0).astype(F32)
    pext_ref[0:HALO, :] = ph_ref[...] * keep
    pext_ref[HALO:HALO + tm, :] = p_ref[...]
    cext_ref[0:HALO, :] = ch_ref[...] * keep
    cext_ref[HALO:HALO + tm, :] = c_ref[...]

    for c in range(tm // CHUNK):
        _mixer_chunk(c, tile, u_ref, vn_ref, p_ref, ws_ref, bst_ref, wp_ref, sp_ref, bdw_ref, lng_ref, lnb_ref,
                     y_ref, pext_ref, cext_ref, shift_ref, wtap_ref, conv_ref,
                     tm=tm, tiles_per_seq=tiles_per_seq, conv_rows=32)


def _mixer_core(u, vn, p, glu, w_s, b_s_t, w_pool, s_pool, w_dw, b_dw, ln_g, ln_b, side, layer, seq, tm=512):
    t = u.shape[0]
    side_in, side_out, side_shape = _side_cast_specs(side, layer, t // tm, col_tile=FFN_CHUNK)
    row = lambda n: pl.BlockSpec((tm, n), lambda i: (i, 0))
    halo = lambda n: pl.BlockSpec((HALO, n), lambda i: (jnp.maximum(i * (tm // HALO) - 1, 0), 0))
    kernel = functools.partial(_mixer_core_kernel, tm=tm, tiles_per_seq=seq // tm)
    return pl.pallas_call(
        kernel,
        grid=(t // tm,),
        in_specs=[
            row(GMLP_WIDTH), row(GMLP_WIDTH),
            row(POOL_WIDTH), halo(POOL_WIDTH),
            row(CONV_WIDTH), halo(CONV_WIDTH),
            _layer_spec(layer, (GMLP_HEADS, CHUNK, CHUNK)),
            _layer_spec(layer, (CHUNK, GMLP_HEADS)),
            _layer_spec(layer, (len(POOL_WINDOWS), POOL_GROUP_WIDTH, POOL_GROUP_WIDTH)),
            _layer_spec(layer, (1, POOL_WIDTH)),
            _layer_spec(layer, (CONV_K, CONV_WIDTH)),
            _layer_spec(layer, (1, CONV_WIDTH)),
            _layer_spec(layer, (1, CONV_WIDTH)),
            _layer_spec(layer, (1, CONV_WIDTH)),
            side_in,
        ],
        out_specs=[row(D_MODEL), side_out],
        out_shape=[jax.ShapeDtypeStruct((t, D_MODEL), BF16), side_shape],
        scratch_shapes=[
            pltpu.VMEM((HALO + tm, POOL_WIDTH), F32),
            pltpu.VMEM((HALO + tm, CONV_WIDTH), F32),
            pltpu.VMEM((SUBLANES - 1, HALO + CHUNK, CONV_WIDTH), F32),
            pltpu.VMEM((CONV_K, SUBLANES, CONV_WIDTH), F32),
            pltpu.VMEM((CHUNK, CONV_WIDTH), F32),
        ],
        compiler_params=_params("parallel"),
        name="mixer_core",
    )(u, vn, p, p, glu, glu, w_s, b_s_t, w_pool, s_pool, w_dw, b_dw, ln_g, ln_b, side)


def _proj_res_kernel(a_ref, x_ref, gpost_ref, gnext_ref, w_hbm, xo_ref, ho_ref, w_ref, stage_ref, sem, *,
                     layer):
    def finish(rows, h):
        xn = x_ref[rows, :] + _rms(h, gpost_ref[...])
        xo_ref[rows, :] = xn
        ho_ref[rows, :] = _rms(xn, gnext_ref[...]).astype(ho_ref.dtype)

    @pl.when(pl.program_id(0) == 0)
    def _():
        finish(slice(None), _load_cast_matmul(a_ref, w_hbm, layer, w_ref, stage_ref, sem))

    @pl.when(pl.program_id(0) > 0)
    def _():
        chunk = a_ref.shape[0] // PROJ_ROW_CHUNKS
        for c in range(PROJ_ROW_CHUNKS):
            rows = slice(c * chunk, (c + 1) * chunk)
            finish(rows, _dot(a_ref[rows, :], w_ref[...]))


def _proj_res(a, w, x, g_post, g_next, layer, tm=512):
    t, k = a.shape
    d = x.shape[1]
    row = lambda c: pl.BlockSpec((tm, c), lambda i: (i, 0))
    return pl.pallas_call(
        functools.partial(_proj_res_kernel, layer=layer),
        grid=(t // tm,),
        in_specs=[row(k), row(d), _layer_spec(layer, (1, d)), _layer_spec(layer, (1, d)), HBM_SPEC],
        out_specs=[row(d), row(d)],
        out_shape=[jax.ShapeDtypeStruct((t, d), F32), jax.ShapeDtypeStruct((t, d), BF16)],
        scratch_shapes=_weight_scratch(k, d),
        compiler_params=_params("arbitrary"),
        name="proj_res",
    )(a, x, g_post, g_next, w)


def _kv_kernel(m_ref, g_ref, wk_ref, wv_ref, k_ref, v_ref, mn_ref):
    @pl.when(pl.program_id(1) == 0)
    def _():
        mn_ref[...] = _rms(m_ref[...], g_ref[...]).astype(BF16)

    k_ref[...] = _dot(mn_ref[...], wk_ref[...].astype(BF16)).astype(k_ref.dtype)
    v_ref[...] = _dot(mn_ref[...], wv_ref[...].astype(BF16)).astype(v_ref.dtype)


def _kv_proj(mem, g_mem, w_k, w_v, tn=256):
    t, d = mem.shape
    depth = w_k.shape[0]
    full = pl.BlockSpec((t, d), lambda l, j: (0, 0))
    gain = pl.BlockSpec((None, 1, d), lambda l, j: (l, 0, 0))
    wcol = pl.BlockSpec((None, d, tn), lambda l, j: (l, 0, j))
    ocol = pl.BlockSpec((None, t, tn), lambda l, j: (l, 0, j))
    return pl.pallas_call(
        _kv_kernel,
        grid=(depth, d // tn),
        in_specs=[full, gain, wcol, wcol],
        out_specs=[ocol, ocol],
        out_shape=[jax.ShapeDtypeStruct((depth, t, d), BF16)] * 2,
        scratch_shapes=[pltpu.VMEM((t, d), BF16)],
        compiler_params=_params("arbitrary", "arbitrary"),
        name="kv_proj",
    )(mem, g_mem, w_k, w_v)


def _attn_kernel(h_ref, k_ref, v_ref, w_hbm, o_ref, wq_ref, stage_ref, sem, *, layer):

    def attend(q):
        q = q.astype(BF16)
        scale = XATTN_HEAD_DIM ** -0.5
        for hd in range(XATTN_HEADS):
            cols = slice(hd * XATTN_HEAD_DIM, (hd + 1) * XATTN_HEAD_DIM)
            s = lax.dot_general(q[:, cols], k_ref[:, cols], (((1,), (1,)), ((), ())),
                                preferred_element_type=F32) * scale
            e = jnp.exp(s - jnp.max(s, axis=-1, keepdims=True))
            o = _dot(e.astype(BF16), v_ref[:, cols]) / jnp.sum(e, axis=-1, keepdims=True)
            o_ref[:, cols] = o.astype(o_ref.dtype)

    @pl.when(pl.program_id(0) == 0)
    def _():
        attend(_load_cast_matmul(h_ref, w_hbm, layer, wq_ref, stage_ref, sem))

    @pl.when(pl.program_id(0) > 0)
    def _():
        attend(_dot(h_ref[...], wq_ref[...]))


def _attention(h, w_q, k, v, layer, seq, mem_len, tm=512):
    t, d = h.shape
    tiles_per_seq = seq // tm
    row = pl.BlockSpec((tm, d), lambda i: (i, 0))
    mem_rows = pl.BlockSpec((None, mem_len, d), lambda i: (layer, i // tiles_per_seq, 0))
    return pl.pallas_call(
        functools.partial(_attn_kernel, layer=layer),
        grid=(t // tm,),
        in_specs=[row, mem_rows, mem_rows, HBM_SPEC],
        out_specs=row,
        out_shape=jax.ShapeDtypeStruct((t, d), BF16),
        scratch_shapes=_weight_scratch(d, d),
        compiler_params=_params("arbitrary"),
        name="attention",
    )(h, k, v, w_q)


def _ffn_kernel(h_ref, gpost_ref, x_hbm, wu_hbm, wd_hbm, xo_ref, wu_buf, wd_buf, sem, x_sem, acc_ref, *,
                n_tiles):
    i = pl.program_id(0)
    n_chunks, _, tf = wu_hbm.shape
    tm = xo_ref.shape[0]

    def copies(f, slot):
        row0 = pl.multiple_of(f * tf, tf)
        return (pltpu.make_async_copy(wu_hbm.at[f], wu_buf.at[slot], sem.at[0, slot]),
                pltpu.make_async_copy(wd_hbm.at[pl.ds(row0, tf), :], wd_buf.at[slot], sem.at[1, slot]))

    def start(f, slot):
        for cp in copies(f, slot):
            cp.start()

    def wait(f, slot):
        for cp in copies(f, slot):
            cp.wait()

    @pl.when(i == 0)
    def _():
        start(0, 0)

    x_copy = pltpu.make_async_copy(x_hbm.at[pl.ds(pl.multiple_of(i * tm, tm), tm), :], xo_ref, x_sem.at[0])
    x_copy.start()

    def hidden(rows, slot):
        a = jnp.square(jnp.maximum(_dot(h_ref[rows, :], wu_buf[slot]), 0.0))
        return _dot(a.astype(BF16), wd_buf[slot])

    def finish(rows, acc):
        xo_ref[rows, :] = xo_ref[rows, :] + _rms(acc, gpost_ref[...])

    def pair(p, carry, first=False, last=False):
        for slot in (0, 1):
            f = 2 * p + slot
            start((f + 1) % n_chunks, 1 - slot)
            wait(f, slot)
            if last and slot == 1:
                x_copy.wait()
                block = tm // FFN_FINAL_ROW_BLOCKS
                for r in range(FFN_FINAL_ROW_BLOCKS):
                    rows = slice(r * block, (r + 1) * block)
                    finish(rows, acc_ref[rows, :] + hidden(rows, slot))
            elif first and slot == 0:
                acc_ref[...] = hidden(slice(None), slot)
            else:
                acc_ref[...] += hidden(slice(None), slot)
        return carry

    n_pairs = n_chunks // 2
    pair(0, 0, first=True)
    lax.fori_loop(1, n_pairs - 1, pair, 0)
    pair(n_pairs - 1, 0, last=True)

    @pl.when(i == n_tiles - 1)
    def _():
        wait(0, 0)


def _ffn(h, w_up, w_down, x, g_post, layer, tm=1024):
    t, d = h.shape
    n = t // tm
    tf = w_up.shape[-1]
    row = pl.BlockSpec((tm, d), lambda i: (i, 0))
    return pl.pallas_call(
        functools.partial(_ffn_kernel, n_tiles=n),
        grid=(n,),
        in_specs=[row, _layer_spec(layer, (1, d)), HBM_SPEC, HBM_SPEC, HBM_SPEC],
        out_specs=row,
        out_shape=jax.ShapeDtypeStruct((t, d), F32),
        scratch_shapes=[pltpu.VMEM((2, d, tf), BF16), pltpu.VMEM((2, tf, d), BF16),
                        pltpu.SemaphoreType.DMA((2, 2)), pltpu.SemaphoreType.DMA((1,)),
                        pltpu.VMEM((tm, d), F32)],
        compiler_params=_params("arbitrary"),
        name="ffn",
    )(h, g_post, x, w_up, w_down)


def kernel(x, mem, norm_mix_pre, norm_mix_post, w_in, w_out, gmlp_v_gain, w_spatial, b_spatial, w_pool,
           s_pool, w_dw, b_dw, conv_ln_g, conv_ln_b, norm_xattn_pre, norm_mem, norm_xattn_post, w_q, w_k,
           w_v, w_o, norm_ffn_pre, norm_ffn_post, w_up, w_down):
    batch, seq, d = x.shape
    mem_len = mem.shape[1]
    depth = w_in.shape[0]
    t = batch * seq

    vec = lambda a: a.reshape(depth, 1, -1)
    norm_mix_pre, norm_mix_post = vec(norm_mix_pre), vec(norm_mix_post)
    norm_xattn_pre, norm_mem, norm_xattn_post = vec(norm_xattn_pre), vec(norm_mem), vec(norm_xattn_post)
    norm_ffn_pre, norm_ffn_post = vec(norm_ffn_pre), vec(norm_ffn_post)
    g_v, s_pool, b_dw = vec(gmlp_v_gain), vec(s_pool), vec(b_dw)
    conv_ln_g, conv_ln_b = vec(conv_ln_g), vec(conv_ln_b)
    b_s_t = jnp.swapaxes(b_spatial, 1, 2)

    xf = x.reshape(t, d)
    memf = mem.reshape(batch * mem_len, d)

    k, v = _kv_proj(memf, norm_mem, w_k, w_v)
    for l in range(depth):
        u, vn, p, glu, w_down_bf = _mixer_in(xf, norm_mix_pre, w_in, g_v, w_down, l)
        y, w_up_bf = _mixer_core(u, vn, p, glu, w_spatial, b_s_t, w_pool, s_pool, w_dw, b_dw, conv_ln_g,
                                 conv_ln_b, w_up, l, seq)
        xf, h = _proj_res(y, w_out, xf, norm_mix_post, norm_xattn_pre, l)
        a = _attention(h, w_q, k, v, l, seq, mem_len)
        xf, h = _proj_res(a, w_o, xf, norm_xattn_post, norm_ffn_pre, l)
        xf = _ffn(h, w_up_bf, w_down_bf, xf, norm_ffn_post, l)
    return xf.reshape(batch, seq, d)
```
